```python
import math
import jax, jax.numpy as jnp
from jax import lax
import numpy as np

D_MODEL = 1024
BATCH = 8
SEQ = 4096
DEPTH = 4

CHUNK = 64
N_MIXERS = 2
N_CONV_LAYERS = (DEPTH + 1) // 2
N_RET_LAYERS = DEPTH // 2

CONV_WIDTH = 31

RET_HEADS = 4
RET_QK_DIM = D_MODEL
RET_V_DIM = 2 * D_MODEL
RET_HEAD_QK = RET_QK_DIM // RET_HEADS
RET_HEAD_V = RET_V_DIM // RET_HEADS
ROPE_BASE = 10000.0

N_GROUPS = 4
EXPERTS_PER_GROUP = 8
N_EXPERTS = N_GROUPS * EXPERTS_PER_GROUP
TOP_K_IN_GROUP = 2
EXPERT_FF = D_MODEL // 2

DEEPNORM_ALPHA = (2.0 * DEPTH) ** 0.25
DEEPNORM_BETA = (8.0 * DEPTH) ** -0.25
LN_EPS = 1e-5

kernel_name = "hybrid_conv_retention_hmoe_deepnorm"


def layer_norm(x, g, b):
    xf = x.astype(jnp.float32)
    mu = jnp.mean(xf, axis=-1, keepdims=True)
    var = jnp.mean(jnp.square(xf - mu), axis=-1, keepdims=True)
    y = (xf - mu) * lax.rsqrt(var + LN_EPS) * g.astype(jnp.float32) + b.astype(jnp.float32)
    return y.astype(x.dtype)


def conformer_conv(x, w_pw1, b_pw1, w_dw, b_dw, ln_g, ln_b, w_pw2, b_pw2):
    h = x @ w_pw1 + b_pw1
    a, gate = jnp.split(h, 2, axis=-1)
    h = a * jax.nn.sigmoid(gate)
    h = lax.conv_general_dilated(
        h, w_dw[:, None, :], window_strides=(1,),
        padding=[(CONV_WIDTH - 1, 0)],
        dimension_numbers=("NWC", "WIO", "NWC"),
        feature_group_count=D_MODEL) + b_dw
    h = jax.nn.silu(layer_norm(h, ln_g, ln_b))
    return h @ w_pw2 + b_pw2


def rotary(t, positions):
    half = t.shape[-1] // 2
    inv_freq = ROPE_BASE ** (-jnp.arange(half, dtype=jnp.float32) / half)
    ang = positions.astype(jnp.float32)[..., None] * inv_freq
    cos = jnp.cos(ang)[:, :, None, :]
    sin = jnp.sin(ang)[:, :, None, :]
    t1, t2 = t[..., :half], t[..., half:]
    return jnp.concatenate([t1 * cos - t2 * sin, t1 * sin + t2 * cos], axis=-1)


def retention(x, positions, w_qkvg, gn_g, gn_b, w_o):
    B, S, _ = x.shape
    H, dk, dv, C = RET_HEADS, RET_HEAD_QK, RET_HEAD_V, CHUNK
    NC = S // C
    qkvg = x @ w_qkvg
    q, k, v, g = jnp.split(qkvg, [RET_QK_DIM, 2 * RET_QK_DIM, 2 * RET_QK_DIM + RET_V_DIM], axis=-1)
    q = rotary(q.reshape(B, S, H, dk).astype(jnp.float32), positions)
    k = rotary(k.reshape(B, S, H, dk).astype(jnp.float32), positions) * (dk ** -0.5)
    v = v.reshape(B, S, H, dv).astype(jnp.float32)

    log_gamma = jnp.log(1.0 - 2.0 ** (-5.0 - jnp.arange(H, dtype=jnp.float32)))
    idx = jnp.arange(C, dtype=jnp.float32)
    d_mask = jnp.exp(log_gamma[:, None, None] * jnp.abs(idx[:, None] - idx[None, :]))
    xi = jnp.exp(log_gamma[None, :] * (idx[:, None] + 1.0))
    zeta = jnp.exp(log_gamma[None, :] * (C - 1.0 - idx[:, None]))
    chunk_decay = jnp.exp(log_gamma * C)

    qc = q.reshape(B, NC, C, H, dk)
    kc = k.reshape(B, NC, C, H, dk)
    vc = v.reshape(B, NC, C, H, dv)

    scores = jnp.einsum("bnihd,bnjhd->bnhij", qc, kc) * d_mask
    o_inner = jnp.einsum("bnhij,bnjhe->bnihe", scores, vc)

    def step(state, inp):
        q_n, k_n, v_n = inp
        o = jnp.einsum("bihd,bhde->bihe", q_n, state) * xi[None, :, :, None]
        state = state * chunk_decay[None, :, None, None] + jnp.einsum(
            "bihd,bihe->bhde", k_n * zeta[None, :, :, None], v_n)
        return state, o

    xs = (qc.transpose(1, 0, 2, 3, 4), kc.transpose(1, 0, 2, 3, 4), vc.transpose(1, 0, 2, 3, 4))
    state0 = jnp.zeros((B, H, dk, dv), jnp.float32)
    _, o_cross = lax.scan(step, state0, xs)
    o = (o_inner + o_cross.transpose(1, 0, 2, 3, 4)).reshape(B, S, H, dv)

    mu = jnp.mean(o, axis=-1, keepdims=True)
    var = jnp.mean(jnp.square(o - mu), axis=-1, keepdims=True)
    o = ((o - mu) * lax.rsqrt(var + LN_EPS)).reshape(B, S, RET_V_DIM)
    o = o * gn_g.astype(jnp.float32) + gn_b.astype(jnp.float32)
    y = (jax.nn.silu(g.astype(jnp.float32)) * o).astype(x.dtype)
    return y @ w_o


def hier_moe(x, w_grp, b_grp, w_route, b_route, w_gate, w_up, w_down):
    B, S, D = x.shape
    T = B * S
    xt = x.reshape(T, D)
    grp_prob = jax.nn.softmax((xt @ w_grp + b_grp).astype(jnp.float32), axis=-1)
    p_g, g_idx = lax.top_k(grp_prob, 1)
    exp_logits = (xt @ w_route + b_route).astype(jnp.float32).reshape(T, N_GROUPS, EXPERTS_PER_GROUP)
    sel = jnp.take_along_axis(exp_logits, g_idx[:, :, None], axis=1)[:, 0]
    top_val, top_idx = lax.top_k(sel, TOP_K_IN_GROUP)
    gates = jax.nn.softmax(top_val, axis=-1) * p_g
    expert_id = g_idx * EXPERTS_PER_GROUP + top_idx
    combine = jnp.sum(jax.nn.one_hot(expert_id, N_EXPERTS, dtype=jnp.float32) * gates[..., None], axis=1)
    combine = combine.astype(x.dtype)
    y = jnp.zeros((T, D), x.dtype)
    for e in range(N_EXPERTS):
        h = jax.nn.silu(xt @ w_gate[e]) * (xt @ w_up[e])
        y = y + combine[:, e:e + 1] * (h @ w_down[e])
    return y.reshape(B, S, D)


def setup_inputs(seed: int = 0) -> dict:
    key = jax.random.key(seed)
    ks = jax.random.split(key, 26)
    D, F = D_MODEL, EXPERT_FF
    nrm = jax.random.normal
    f32 = jnp.float32
    x = nrm(ks[0], (BATCH, SEQ, D), f32)
    offset = jax.random.randint(ks[1], (BATCH, 1), 0, 4096, dtype=jnp.int32)
    positions = (offset + jnp.arange(SEQ, dtype=jnp.int32)[None, :]).astype(jnp.int32)
    Lc, Lr = N_CONV_LAYERS, N_RET_LAYERS
    return {
        "x": x,
        "positions": positions,
        "conv_w_pw1": nrm(ks[2], (Lc, D, 2 * D), f32) * D ** -0.5,
        "conv_b_pw1": nrm(ks[3], (Lc, 2 * D), f32) * 0.02,
        "conv_w_dw": nrm(ks[4], (Lc, CONV_WIDTH, D), f32) * CONV_WIDTH ** -0.5,
        "conv_b_dw": nrm(ks[5], (Lc, D), f32) * 0.02,
        "conv_ln_g": 1.0 + 0.05 * nrm(ks[6], (Lc, D), f32),
        "conv_ln_b": nrm(ks[7], (Lc, D), f32) * 0.02,
        "conv_w_pw2": nrm(ks[8], (Lc, D, D), f32) * D ** -0.5 * DEEPNORM_BETA,
        "conv_b_pw2": nrm(ks[9], (Lc, D), f32) * 0.02,
        "ret_w_qkvg": nrm(ks[10], (Lr, D, 2 * RET_QK_DIM + 2 * RET_V_DIM), f32) * D ** -0.5,
        "ret_gn_g": 1.0 + 0.05 * nrm(ks[11], (Lr, RET_V_DIM), f32),
        "ret_gn_b": nrm(ks[12], (Lr, RET_V_DIM), f32) * 0.02,
        "ret_w_o": nrm(ks[13], (Lr, RET_V_DIM, D), f32) * RET_V_DIM ** -0.5 * DEEPNORM_BETA,
        "ln1_g": 1.0 + 0.05 * nrm(ks[14], (DEPTH, D), f32),
        "ln1_b": nrm(ks[15], (DEPTH, D), f32) * 0.02,
        "ln2_g": 1.0 + 0.05 * nrm(ks[16], (DEPTH, D), f32),
        "ln2_b": nrm(ks[17], (DEPTH, D), f32) * 0.02,
        "moe_w_grp": nrm(ks[18], (DEPTH, D, N_GROUPS), f32) * D ** -0.5,
        "moe_b_grp": nrm(ks[19], (DEPTH, N_GROUPS), f32) * 0.01,
        "moe_w_route": nrm(ks[20], (DEPTH, D, N_EXPERTS), f32) * D ** -0.5,
        "moe_b_route": nrm(ks[21], (DEPTH, N_EXPERTS), f32) * 0.01,
        "moe_w_gate": nrm(ks[22], (DEPTH, N_EXPERTS, D, F), f32) * D ** -0.5,
        "moe_w_up": nrm(ks[23], (DEPTH, N_EXPERTS, D, F), f32) * D ** -0.5,
        "moe_w_down": nrm(ks[24], (DEPTH, N_EXPERTS, F, D), f32) * F ** -0.5 * DEEPNORM_BETA,
    }


def reference(x, positions, conv_w_pw1, conv_b_pw1, conv_w_dw, conv_b_dw, conv_ln_g, conv_ln_b,
              conv_w_pw2, conv_b_pw2, ret_w_qkvg, ret_gn_g, ret_gn_b, ret_w_o,
              ln1_g, ln1_b, ln2_g, ln2_b, moe_w_grp, moe_b_grp, moe_w_route, moe_b_route,
              moe_w_gate, moe_w_up, moe_w_down):
    for i in range(DEPTH):
        j = i // N_MIXERS
        if i % N_MIXERS == 0:
            mix = conformer_conv(x, conv_w_pw1[j], conv_b_pw1[j], conv_w_dw[j], conv_b_dw[j],
                                 conv_ln_g[j], conv_ln_b[j], conv_w_pw2[j], conv_b_pw2[j])
        else:
            mix = retention(x, positions, ret_w_qkvg[j], ret_gn_g[j], ret_gn_b[j], ret_w_o[j])
        x = layer_norm(DEEPNORM_ALPHA * x + mix, ln1_g[i], ln1_b[i])
        ffn = hier_moe(x, moe_w_grp[i], moe_b_grp[i], moe_w_route[i], moe_b_route[i],
                       moe_w_gate[i], moe_w_up[i], moe_w_down[i])
        x = layer_norm(DEEPNORM_ALPHA * x + ffn, ln2_g[i], ln2_b[i])
    return x
```

```python
import functools
import math

import jax
import jax.numpy as jnp
from jax import lax
from jax.experimental import pallas as pl
from jax.experimental.pallas import tpu as pltpu

F32 = jnp.float32
BF16 = jnp.bfloat16
I32 = jnp.int32

DEPTH = 4
N_MIXERS = 2
CONV_WIDTH = 31
RET_HEADS = 4
N_GROUPS = 4
EXPERTS_PER_GROUP = 8
N_EXPERTS = N_GROUPS * EXPERTS_PER_GROUP
ROPE_BASE = 10000.0
DEEPNORM_ALPHA = (2.0 * DEPTH) ** 0.25
LN_EPS = 1e-5

LANES = 128
SUBLANES = 8
VMEM_LIMIT = 56 * 1024 * 1024

HALO = 32
CONV_ROWS = 32
TS_TAIL = 256
TM_PW1 = 512
TM_QKVG = 256
RET_BLOCK = 256
RET_SUPER = 1024
TM_FFN = 256
G_DISPATCH = 512
G_COMBINE = 256
ROUTE_LANE0 = N_GROUPS


def _cparams(sem):
    return pltpu.CompilerParams(dimension_semantics=sem, vmem_limit_bytes=VMEM_LIMIT)


def _ln(x, g, b):
    mu = jnp.mean(x, axis=-1, keepdims=True)
    xc = x - mu
    var = jnp.mean(xc * xc, axis=-1, keepdims=True)
    return xc * lax.rsqrt(var + LN_EPS) * g + b


def _silu(x):
    return x * jax.nn.sigmoid(x)


def _store_rows_tiled(o_ref, val):
    m = val.shape[0]
    for c in range(val.shape[1] // LANES):
        o_ref[pl.ds(c, m, stride=SUBLANES), :] = val[:, c * LANES:(c + 1) * LANES]


def _load_rows_tiled(ref, m, base=0):
    return [ref[pl.ds(base + c, m, stride=SUBLANES), :] for c in range(SUBLANES)]


def _ln1_route(pre, l1g_ref, l1b_ref, wrh_ref, wrl_ref, br_ref, base_ref, x1_ref, ri_ref, rg_ref, cnt_ref):
    ts = pre.shape[0]
    x1 = _ln(pre, l1g_ref[...], l1b_ref[...])
    _store_rows_tiled(x1_ref, x1)

    xh = x1.astype(BF16)
    xl = (x1 - xh.astype(F32)).astype(BF16)
    wrh = wrh_ref[...]
    logits = (jnp.dot(xh, wrh, preferred_element_type=F32)
              + jnp.dot(xl, wrh, preferred_element_type=F32)
              + jnp.dot(xh, wrl_ref[...], preferred_element_type=F32)) + br_ref[...]

    col = lax.broadcasted_iota(I32, logits.shape, 1).astype(F32)
    neg = jnp.float32(-jnp.inf)
    no_lane = jnp.float32(LANES)
    is_grp = col < N_GROUPS
    gl = jnp.where(is_grp, logits, neg)
    gm = jnp.max(gl, axis=-1, keepdims=True)
    gidx = jnp.min(jnp.where(gl == gm, col, no_lane), axis=-1, keepdims=True)
    denom = jnp.sum(jnp.where(is_grp, jnp.exp(gl - gm), 0.0), axis=-1, keepdims=True)
    p_g = 1.0 / denom

    lo = ROUTE_LANE0 + EXPERTS_PER_GROUP * gidx
    sel = (col >= lo) & (col < lo + EXPERTS_PER_GROUP)
    sl = jnp.where(sel, logits, neg)
    m1 = jnp.max(sl, axis=-1, keepdims=True)
    i1 = jnp.min(jnp.where(sl == m1, col, no_lane), axis=-1, keepdims=True)
    sl2 = jnp.where(col == i1, neg, sl)
    m2 = jnp.max(sl2, axis=-1, keepdims=True)
    i2 = jnp.min(jnp.where(sl2 == m2, col, no_lane), axis=-1, keepdims=True)
    e21 = jnp.exp(m2 - m1)
    g0 = p_g / (1.0 + e21)
    g1 = p_g * e21 / (1.0 + e21)

    oh0 = col == i1
    oh1 = col == i2
    s_f = jnp.where(oh0 | oh1, 1.0, 0.0)
    rr = lax.broadcasted_iota(I32, (ts, ts), 0)
    cc = lax.broadcasted_iota(I32, (ts, ts), 1)
    tri = jnp.where(cc < rr, 1.0, 0.0).astype(BF16)
    before = jnp.dot(tri, s_f.astype(BF16), preferred_element_type=F32) + base_ref[...]
    r0 = jnp.sum(jnp.where(oh0, before, 0.0), axis=-1, keepdims=True)
    r1 = jnp.sum(jnp.where(oh1, before, 0.0), axis=-1, keepdims=True)
    base_ref[...] += jnp.sum(s_f, axis=0, keepdims=True)
    cnt_ref[...] = base_ref[...]

    ri = jnp.where(col == 0, i1 - ROUTE_LANE0,
                   jnp.where(col == 1, i2 - ROUTE_LANE0,
                             jnp.where(col == 2, r0, jnp.where(col == 3, r1, 0.0))))
    ri_ref[...] = ri.astype(I32)
    rg_ref[...] = jnp.where(col == 0, g0, jnp.where(col == 1, g1, 0.0))


def _tail_out_shapes(t, d):
    return (jax.ShapeDtypeStruct((t * SUBLANES, LANES), F32),
            jax.ShapeDtypeStruct((t, LANES), I32),
            jax.ShapeDtypeStruct((t, LANES), F32),
            jax.ShapeDtypeStruct((1, LANES), F32))


def _pw1_glu_kernel(x_ref, w_ref, b_ref, o_ref):
    d = o_ref.shape[-1]
    h = jnp.dot(x_ref[...].astype(BF16), w_ref[...], preferred_element_type=F32) + b_ref[...]
    o_ref[...] = h[:, :d] * jax.nn.sigmoid(h[:, d:])


def _pw1_glu(x, w_bf, b):
    t, d = x.shape
    tm = TM_PW1
    return pl.pallas_call(
        _pw1_glu_kernel,
        grid=(t // tm,),
        in_specs=[pl.BlockSpec((tm, d), lambda i: (i, 0)),
                  pl.BlockSpec((d, 2 * d), lambda i: (0, 0)),
                  pl.BlockSpec((1, 2 * d), lambda i: (0, 0))],
        out_specs=pl.BlockSpec((tm, d), lambda i: (i, 0)),
        out_shape=jax.ShapeDtypeStruct((t, d), F32),
        compiler_params=_cparams(("arbitrary",)),
        name="conv_pw1_glu",
    )(x, w_bf, b.reshape(1, -1))


def _conv_tail_kernel(hcur_ref, hprev_ref, x_ref, wdw_ref, bdw_ref, lng_ref, lnb_ref, wpw2_ref, bpw2_ref,
                      l1g_ref, l1b_ref, wrh_ref, wrl_ref, br_ref,
                      x1_ref, ri_ref, rg_ref, cnt_ref,
                      hext_ref, hsh_ref, conv_ref, base_ref):
    ts, d = hcur_ref.shape
    b_id = pl.program_id(0)
    j = pl.program_id(1)

    @pl.when((b_id == 0) & (j == 0))
    def _():
        base_ref[...] = jnp.zeros_like(base_ref)

    hext_ref[HALO:, :] = hcur_ref[...]

    @pl.when(j == 0)
    def _():
        hext_ref[0:HALO, :] = jnp.zeros((HALO, d), F32)

    @pl.when(j > 0)
    def _():
        hext_ref[0:HALO, :] = hprev_ref[...]

    n_sh = ts + HALO - SUBLANES
    for b in range(1, SUBLANES):
        hsh_ref[b - 1] = hext_ref[b:b + n_sh, :]
    offs = [HALO - (CONV_WIDTH - 1) + k for k in range(CONV_WIDTH)]

    for c in range(d // LANES):
        lanes = slice(c * LANES, (c + 1) * LANES)
        w_rows = [jnp.broadcast_to(wdw_ref[k:k + 1, lanes], (SUBLANES, LANES)) for k in range(CONV_WIDTH)]
        bias = jnp.broadcast_to(bdw_ref[:, lanes], (SUBLANES, LANES))

        def conv_rows(r, carry, lanes=lanes, w_rows=w_rows, bias=bias):
            r0 = pl.multiple_of(r * CONV_ROWS, CONV_ROWS)
            for grp in range(CONV_ROWS // SUBLANES):
                acc = bias
                for k, off in enumerate(offs):
                    start = r0 + (grp + off // SUBLANES) * SUBLANES
                    if off % SUBLANES == 0:
                        tap = hext_ref[pl.ds(start, SUBLANES), lanes]
                    else:
                        tap = hsh_ref[off % SUBLANES - 1, pl.ds(start, SUBLANES), lanes]
                    acc = acc + w_rows[k] * tap
                conv_ref[pl.ds(r0 + grp * SUBLANES, SUBLANES), lanes] = acc
            return carry

        lax.fori_loop(0, ts // CONV_ROWS, conv_rows, 0)

    hn = _silu(_ln(conv_ref[...], lng_ref[...], lnb_ref[...]))
    mix = jnp.dot(hn.astype(BF16), wpw2_ref[...], preferred_element_type=F32) + bpw2_ref[...]
    pre = DEEPNORM_ALPHA * x_ref[...] + mix
    _ln1_route(pre, l1g_ref, l1b_ref, wrh_ref, wrl_ref, br_ref, base_ref, x1_ref, ri_ref, rg_ref, cnt_ref)


def _conv_tail(h, x, batch, seq, w_dw, b_dw, ln_g, ln_b, w_pw2_bf, b_pw2, l1g, l1b, wrh, wrl, br):
    t, d = x.shape
    ts = TS_TAIL
    nj = seq // ts
    halo_per_tile = ts // HALO
    h3 = h.reshape(batch, seq, d)
    x3 = x.reshape(batch, seq, d)
    row = lambda v: v.reshape(1, -1)
    const2 = lambda shape: pl.BlockSpec(shape, lambda b, j: (0, 0))
    tok = lambda b, j: (b * nj + j, 0)
    return pl.pallas_call(
        _conv_tail_kernel,
        grid=(batch, nj),
        in_specs=[pl.BlockSpec((None, ts, d), lambda b, j: (b, j, 0)),
                  pl.BlockSpec((None, HALO, d), lambda b, j: (b, jnp.maximum(j * halo_per_tile - 1, 0), 0)),
                  pl.BlockSpec((None, ts, d), lambda b, j: (b, j, 0)),
                  const2((CONV_WIDTH, d)), const2((1, d)), const2((1, d)), const2((1, d)),
                  const2((d, d)), const2((1, d)), const2((1, d)), const2((1, d)),
                  const2((d, LANES)), const2((d, LANES)), const2((1, LANES))],
        out_specs=(pl.BlockSpec((ts * SUBLANES, LANES), tok),
                   pl.BlockSpec((ts, LANES), tok),
                   pl.BlockSpec((ts, LANES), tok),
                   pl.BlockSpec((1, LANES), lambda b, j: (0, 0))),
        out_shape=_tail_out_shapes(t, d),
        scratch_shapes=[pltpu.VMEM((ts + HALO, d), F32), pltpu.VMEM((SUBLANES - 1, ts + HALO - SUBLANES, d), F32),
                        pltpu.VMEM((ts, d), F32), pltpu.VMEM((1, LANES), F32)],
        compiler_params=_cparams(("arbitrary", "arbitrary")),
        name="conv_tail_ln1_route",
    )(h3, h3, x3, w_dw, row(b_dw), row(ln_g), row(ln_b), w_pw2_bf, row(b_pw2), row(l1g), row(l1b), wrh, wrl, br)


def _rope_table_kernel(pos_ref, invf_ref, cos_ref, sin_ref):
    ang = pos_ref[...].astype(F32) * invf_ref[...]
    cos_ref[...] = jnp.cos(ang)
    sin_ref[...] = jnp.sin(ang)


def _rope_tables(positions, half):
    t = positions.size
    tm = 1024
    inv_freq = ROPE_BASE ** (-jnp.arange(half, dtype=F32) / half)
    return pl.pallas_call(
        _rope_table_kernel,
        grid=(t // tm,),
        in_specs=[pl.BlockSpec((tm, 1), lambda i: (i, 0)), pl.BlockSpec((1, half), lambda i: (0, 0))],
        out_specs=(pl.BlockSpec((tm, half), lambda i: (i, 0)), pl.BlockSpec((tm, half), lambda i: (i, 0))),
        out_shape=(jax.ShapeDtypeStruct((t, half), F32), jax.ShapeDtypeStruct((t, half), F32)),
        compiler_params=_cparams(("arbitrary",)),
        name="rope_tables",
    )(positions.reshape(t, 1), inv_freq.reshape(1, half))


def _rotate(t, cos, sin, head_dim):
    half = head_dim // 2
    parts = []
    for h in range(t.shape[1] // head_dim):
        t1 = t[:, h * head_dim:h * head_dim + half]
        t2 = t[:, h * head_dim + half:(h + 1) * head_dim]
        parts.append(t1 * cos - t2 * sin)
        parts.append(t1 * sin + t2 * cos)
    return jnp.concatenate(parts, axis=-1)


def _qkvg_kernel(x_ref, wq_ref, wk_ref, wv_ref, wg_ref, cos_ref, sin_ref, q_ref, k_ref, v_ref, g_ref, *, head_qk):
    xb = x_ref[...].astype(BF16)
    cos = cos_ref[...]
    sin = sin_ref[...]
    q = jnp.dot(xb, wq_ref[...], preferred_element_type=F32)
    q_ref[...] = _rotate(q, cos, sin, head_qk).astype(BF16)
    k = jnp.dot(xb, wk_ref[...], preferred_element_type=F32)
    k_ref[...] = (_rotate(k, cos, sin, head_qk) * (head_qk ** -0.5)).astype(BF16)
    v_ref[...] = jnp.dot(xb, wv_ref[...], preferred_element_type=F32).astype(BF16)
    g_ref[...] = jnp.dot(xb, wg_ref[...], preferred_element_type=F32)


def _qkvg(x, wq, wk, wv, wg, cos, sin):
    t, d = x.shape
    qk = wq.shape[1]
    vd = wv.shape[1]
    head_qk = qk // RET_HEADS
    tm = TM_QKVG
    tokb = lambda n: pl.BlockSpec((tm, n), lambda i: (i, 0))
    wspec = lambda n: pl.BlockSpec((d, n), lambda i: (0, 0))
    return pl.pallas_call(
        functools.partial(_qkvg_kernel, head_qk=head_qk),
        grid=(t // tm,),
        in_specs=[tokb(d), wspec(qk), wspec(qk), wspec(vd), wspec(vd), tokb(head_qk // 2), tokb(head_qk // 2)],
        out_specs=(tokb(qk), tokb(qk), tokb(vd), tokb(vd)),
        out_shape=(jax.ShapeDtypeStruct((t, qk), BF16), jax.ShapeDtypeStruct((t, qk), BF16),
                   jax.ShapeDtypeStruct((t, vd), BF16), jax.ShapeDtypeStruct((t, vd), F32)),
        compiler_params=_cparams(("arbitrary",)),
        name="ret_qkvg_rope",
    )(x, wq, wk, wv, wg, cos, sin)


def _ret_core_kernel(q_ref, k_ref, v_ref, g_ref, mask_ref, xi_ref, zeta_ref, dec_ref, gng_ref, gnb_ref,
                     y_ref, state_ref):
    blk = RET_BLOCK
    n_blk = q_ref.shape[0] // blk

    @pl.when(pl.program_id(2) == 0)
    def _():
        state_ref[...] = jnp.zeros_like(state_ref)

    mask = mask_ref[...]
    xi = xi_ref[...]
    zeta = zeta_ref[...]
    dec = dec_ref[0:1, 0:1]
    for n in range(n_blk):
        rows = slice(n * blk, (n + 1) * blk)
        q = q_ref[rows, :]
        k = k_ref[rows, :]
        v = v_ref[rows, :]
        state = state_ref[...]
        s = lax.dot_general(q, k, (((1,), (1,)), ((), ())), preferred_element_type=F32)
        p = (s * mask).astype(BF16)
        o = jnp.dot(p, v, preferred_element_type=F32)
        qx = (q.astype(F32) * xi).astype(BF16)
        o = o + jnp.dot(qx, state.astype(BF16), preferred_element_type=F32)
        kz = (k.astype(F32) * zeta).astype(BF16)
        state_ref[...] = state * dec + lax.dot_general(kz, v, (((0,), (0,)), ((), ())),
                                                       preferred_element_type=F32)
        mu = jnp.mean(o, axis=-1, keepdims=True)
        oc = o - mu
        var = jnp.mean(oc * oc, axis=-1, keepdims=True)
        on = oc * lax.rsqrt(var + LN_EPS) * gng_ref[...] + gnb_ref[...]
        y_ref[rows, :] = (_silu(g_ref[rows, :]) * on).astype(BF16)


def _ret_tables(head_qk):
    del head_qk
    blk = RET_BLOCK
    chunk = 64
    log_gamma = jnp.log(1.0 - 2.0 ** (-5.0 - jnp.arange(RET_HEADS, dtype=F32)))
    idx = jnp.arange(blk, dtype=F32)
    dist = jnp.abs(idx[:, None] - idx[None, :])
    visible = (jnp.floor(idx[None, :] / chunk) <= jnp.floor(idx[:, None] / chunk))
    mask = jnp.where(visible[None], jnp.exp(log_gamma[:, None, None] * dist[None]), 0.0)
    xi = jnp.exp(log_gamma[:, None] * (idx[None, :] + 1.0))[..., None]
    zeta = jnp.exp(log_gamma[:, None] * (blk - 1.0 - idx[None, :]))[..., None]
    dec = jnp.broadcast_to(jnp.exp(log_gamma * blk)[:, None, None], (RET_HEADS, SUBLANES, LANES))
    return mask.astype(F32), xi.astype(F32), zeta.astype(F32), dec.astype(F32)


def _ret_core(q, k, v, g, gn_g, gn_b, batch, seq):
    t, qk = q.shape
    vd = v.shape[1]
    hq = qk // RET_HEADS
    hv = vd // RET_HEADS
    sb = RET_SUPER
    ns = seq // sb
    mask, xi, zeta, dec = _ret_tables(hq)
    tokb = lambda n: pl.BlockSpec((sb, n), lambda b, h, s: (b * ns + s, h))
    headb = lambda r, c: pl.BlockSpec((None, r, c), lambda b, h, s: (h, 0, 0))
    return pl.pallas_call(
        _ret_core_kernel,
        grid=(batch, RET_HEADS, ns),
        in_specs=[tokb(hq), tokb(hq), tokb(hv), tokb(hv),
                  headb(RET_BLOCK, RET_BLOCK), headb(RET_BLOCK, 1), headb(RET_BLOCK, 1), headb(SUBLANES, LANES),
                  pl.BlockSpec((1, hv), lambda b, h, s: (0, h)), pl.BlockSpec((1, hv), lambda b, h, s: (0, h))],
        out_specs=tokb(hv),
        out_shape=jax.ShapeDtypeStruct((t, vd), BF16),
        scratch_shapes=[pltpu.VMEM((hq, hv), F32)],
        compiler_params=_cparams(("arbitrary", "arbitrary", "arbitrary")),
        name="ret_core",
    )(q, k, v, g, mask, xi, zeta, dec, gn_g.reshape(1, -1), gn_b.reshape(1, -1))


def _ret_tail_kernel(y_ref, x_ref, wo_ref, l1g_ref, l1b_ref, wrh_ref, wrl_ref, br_ref,
                     x1_ref, ri_ref, rg_ref, cnt_ref, base_ref):
    @pl.when(pl.program_id(0) == 0)
    def _():
        base_ref[...] = jnp.zeros_like(base_ref)

    mix = jnp.dot(y_ref[...], wo_ref[...], preferred_element_type=F32)
    pre = DEEPNORM_ALPHA * x_ref[...] + mix
    _ln1_route(pre, l1g_ref, l1b_ref, wrh_ref, wrl_ref, br_ref, base_ref, x1_ref, ri_ref, rg_ref, cnt_ref)


def _ret_tail(y, x, w_o_bf, l1g, l1b, wrh, wrl, br):
    t, d = x.shape
    vd = y.shape[1]
    ts = TS_TAIL
    row = lambda v: v.reshape(1, -1)
    const = lambda shape: pl.BlockSpec(shape, lambda i: (0, 0))
    tok = lambda i: (i, 0)
    return pl.pallas_call(
        _ret_tail_kernel,
        grid=(t // ts,),
        in_specs=[pl.BlockSpec((ts, vd), tok), pl.BlockSpec((ts, d), tok), const((vd, d)),
                  const((1, d)), const((1, d)), const((d, LANES)), const((d, LANES)), const((1, LANES))],
        out_specs=(pl.BlockSpec((ts * SUBLANES, LANES), tok),
                   pl.BlockSpec((ts, LANES), tok),
                   pl.BlockSpec((ts, LANES), tok),
                   pl.BlockSpec((1, LANES), lambda i: (0, 0))),
        out_shape=_tail_out_shapes(t, d),
        scratch_shapes=[pltpu.VMEM((1, LANES), F32)],
        compiler_params=_cparams(("arbitrary",)),
        name="ret_tail_ln1_route",
    )(y, x, w_o_bf, row(l1g), row(l1b), wrh, wrl, br)


def _dispatch_kernel(pos_ref, x1_hbm, xs_init_hbm, xs_hbm, sem):
    del xs_init_hbm
    g = pos_ref.shape[-1] // 2
    i = pl.program_id(0)

    def row(ref, r):
        return ref.at[pl.ds(pl.multiple_of(r * SUBLANES, SUBLANES), SUBLANES)]

    def issue(j, carry):
        src = row(x1_hbm, i * g + j)
        pltpu.make_async_copy(src, row(xs_hbm, pos_ref[0, 0, j]), sem.at[0]).start()
        pltpu.make_async_copy(src, row(xs_hbm, pos_ref[0, 0, g + j]), sem.at[0]).start()
        return carry

    lax.fori_loop(0, g, issue, 0, unroll=8)
    n = 2 * g * SUBLANES
    pltpu.make_async_copy(x1_hbm.at[pl.ds(0, n)], xs_hbm.at[pl.ds(0, n)], sem.at[0]).wait()


def _dispatch(x1_2d, pos, n_rows):
    t = x1_2d.shape[0] // SUBLANES
    g = G_DISPATCH
    pos_blocks = pos.reshape(t // g, g, 2).transpose(0, 2, 1).reshape(t // g, 1, 2 * g)
    xs_init = jnp.zeros((n_rows * SUBLANES, LANES), F32)
    return pl.pallas_call(
        _dispatch_kernel,
        grid=(t // g,),
        in_specs=[pl.BlockSpec((1, 1, 2 * g), lambda i: (i, 0, 0), memory_space=pltpu.SMEM),
                  pl.BlockSpec(memory_space=pl.ANY),
                  pl.BlockSpec(memory_space=pl.ANY)],
        out_specs=pl.BlockSpec(memory_space=pl.ANY),
        out_shape=jax.ShapeDtypeStruct((n_rows * SUBLANES, LANES), F32),
        scratch_shapes=[pltpu.SemaphoreType.DMA((1,))],
        input_output_aliases={2: 0},
        compiler_params=_cparams(("arbitrary",)),
        name="moe_dispatch",
    )(pos_blocks, x1_2d, xs_init)


def _ffn_kernel(tile_e_ref, nt_ref, x_ref, wg_ref, wu_ref, wd_ref, o_ref):
    del tile_e_ref
    tm = o_ref.shape[0] // SUBLANES
    i = pl.program_id(0)

    @pl.when(i < nt_ref[0])
    def _():
        x = jnp.concatenate([c.astype(BF16) for c in _load_rows_tiled(x_ref, tm)], axis=-1)
        a = jnp.dot(x, wg_ref[...], preferred_element_type=F32)
        u = jnp.dot(x, wu_ref[...], preferred_element_type=F32)
        h = (_silu(a) * u).astype(BF16)
        _store_rows_tiled(o_ref, jnp.dot(h, wd_ref[...], preferred_element_type=F32))

    @pl.when(i >= nt_ref[0])
    def _():
        o_ref[...] = jnp.zeros_like(o_ref)


def _ffn(xs, tile_e, n_tiles, wg_bf, wu_bf, wd_bf):
    n_rows = xs.shape[0] // SUBLANES
    tm = TM_FFN
    nt_max = n_rows // tm
    _, d, f = wg_bf.shape
    grid_spec = pltpu.PrefetchScalarGridSpec(
        num_scalar_prefetch=2,
        grid=(nt_max,),
        in_specs=[pl.BlockSpec((tm * SUBLANES, LANES), lambda i, te, nt: (jnp.minimum(i, nt[0] - 1), 0)),
                  pl.BlockSpec((None, d, f), lambda i, te, nt: (te[i], 0, 0)),
                  pl.BlockSpec((None, d, f), lambda i, te, nt: (te[i], 0, 0)),
                  pl.BlockSpec((None, f, d), lambda i, te, nt: (te[i], 0, 0))],
        out_specs=pl.BlockSpec((tm * SUBLANES, LANES), lambda i, te, nt: (i, 0)),
    )
    return pl.pallas_call(
        _ffn_kernel,
        grid_spec=grid_spec,
        out_shape=jax.ShapeDtypeStruct((n_rows * SUBLANES, LANES), F32),
        compiler_params=_cparams(("arbitrary",)),
        name="moe_expert_ffn",
    )(tile_e, n_tiles, xs, wg_bf, wu_bf, wd_bf)


def _combine_ln2_kernel(pos_ref, x1_ref, rg_ref, l2g_ref, l2b_ref, ys_hbm, o_ref, ybuf_ref, sem):
    g = o_ref.shape[0]

    def row(ref, r):
        return ref.at[pl.ds(pl.multiple_of(r * SUBLANES, SUBLANES), SUBLANES)]

    def issue(j, carry):
        pltpu.make_async_copy(row(ys_hbm, pos_ref[0, 0, j]), row(ybuf_ref, j), sem.at[0]).start()
        pltpu.make_async_copy(row(ys_hbm, pos_ref[0, 0, g + j]), row(ybuf_ref, g + j), sem.at[0]).start()
        return carry

    lax.fori_loop(0, g, issue, 0, unroll=8)
    pltpu.make_async_copy(ys_hbm.at[pl.ds(0, 2 * g * SUBLANES)], ybuf_ref, sem.at[0]).wait()

    gates = rg_ref[...]
    g0 = gates[:, 0:1]
    g1 = gates[:, 1:2]
    x1c = _load_rows_tiled(x1_ref, g)
    y0c = _load_rows_tiled(ybuf_ref, g)
    y1c = _load_rows_tiled(ybuf_ref, g, base=g * SUBLANES)
    pre = jnp.concatenate([DEEPNORM_ALPHA * a + (g0 * b + g1 * c) for a, b, c in zip(x1c, y0c, y1c)], axis=-1)
    o_ref[...] = _ln(pre, l2g_ref[...], l2b_ref[...])


def _combine_ln2(x1_2d, rg, pos, ys_tiled, l2g, l2b):
    t = rg.shape[0]
    d = l2g.shape[0]
    g = G_COMBINE
    pos_blocks = pos.reshape(t // g, g, 2).transpose(0, 2, 1).reshape(t // g, 1, 2 * g)
    return pl.pallas_call(
        _combine_ln2_kernel,
        grid=(t // g,),
        in_specs=[pl.BlockSpec((1, 1, 2 * g), lambda i: (i, 0, 0), memory_space=pltpu.SMEM),
                  pl.BlockSpec((g * SUBLANES, LANES), lambda i: (i, 0)),
                  pl.BlockSpec((g, LANES), lambda i: (i, 0)),
                  pl.BlockSpec((1, d), lambda i: (0, 0)),
                  pl.BlockSpec((1, d), lambda i: (0, 0)),
                  pl.BlockSpec(memory_space=pl.ANY)],
        out_specs=pl.BlockSpec((g, d), lambda i: (i, 0)),
        out_shape=jax.ShapeDtypeStruct((t, d), F32),
        scratch_shapes=[pltpu.VMEM((2 * g * SUBLANES, LANES), F32), pltpu.SemaphoreType.DMA((1,))],
        compiler_params=_cparams(("arbitrary",)),
        name="moe_combine_ln2",
    )(pos_blocks, x1_2d, rg, l2g.reshape(1, -1), l2b.reshape(1, -1), ys_tiled)


def _moe(x1_2d, ri, rg, cnt, wg_bf, wu_bf, wd_bf, l2g, l2b):
    t = ri.shape[0]
    tm = TM_FFN
    n_rows = ((2 * t + N_EXPERTS * (tm - 1) + tm - 1) // tm) * tm
    counts = cnt[0, ROUTE_LANE0:ROUTE_LANE0 + N_EXPERTS].astype(I32)
    padded = ((counts + tm - 1) // tm) * tm
    ends = jnp.cumsum(padded)
    offs = ends - padded
    pos = jnp.take(offs, ri[:, 0:2]) + ri[:, 2:4]
    tile_start = jnp.arange(n_rows // tm, dtype=I32) * tm
    tile_e = jnp.minimum(jnp.searchsorted(ends, tile_start, side="right"), N_EXPERTS - 1).astype(I32)
    n_tiles = (ends[-1:] // tm).astype(I32)

    xs = _dispatch(x1_2d, pos, n_rows)
    ys = _ffn(xs, tile_e, n_tiles, wg_bf, wu_bf, wd_bf)
    return _combine_ln2(x1_2d, rg, pos, ys, l2g, l2b)


def _router_weights(w_grp, b_grp, w_route, b_route):
    d = w_grp.shape[0]
    used = N_GROUPS + N_EXPERTS
    w = jnp.concatenate([w_grp, w_route, jnp.zeros((d, LANES - used), F32)], axis=1)
    b = jnp.concatenate([b_grp, b_route, jnp.zeros((LANES - used,), F32)]).reshape(1, LANES)
    wh = w.astype(BF16)
    wl = (w - wh.astype(F32)).astype(BF16)
    return wh, wl, b


def kernel(x, positions, conv_w_pw1, conv_b_pw1, conv_w_dw, conv_b_dw, conv_ln_g, conv_ln_b, conv_w_pw2, conv_b_pw2,
           ret_w_qkvg, ret_gn_g, ret_gn_b, ret_w_o, ln1_g, ln1_b, ln2_g, ln2_b, moe_w_grp, moe_b_grp, moe_w_route,
           moe_b_route, moe_w_gate, moe_w_up, moe_w_down):
    batch, seq, d = x.shape
    t = batch * seq
    qk = d
    vd = 2 * d
    xt = x.reshape(t, d)
    cos, sin = _rope_tables(positions, qk // RET_HEADS // 2)
    for i in range(DEPTH):
        j = i // N_MIXERS
        wrh, wrl, br = _router_weights(moe_w_grp[i], moe_b_grp[i], moe_w_route[i], moe_b_route[i])
        if i % N_MIXERS == 0:
            h = _pw1_glu(xt, conv_w_pw1[j].astype(BF16), conv_b_pw1[j])
            x1, ri, rg, cnt = _conv_tail(h, xt, batch, seq, conv_w_dw[j], conv_b_dw[j], conv_ln_g[j], conv_ln_b[j],
                                         conv_w_pw2[j].astype(BF16), conv_b_pw2[j], ln1_g[i], ln1_b[i], wrh, wrl, br)
        else:
            w = ret_w_qkvg[j].astype(BF16)
            q, k, v, g = _qkvg(xt, w[:, :qk], w[:, qk:2 * qk], w[:, 2 * qk:2 * qk + vd], w[:, 2 * qk + vd:], cos, sin)
            y = _ret_core(q, k, v, g, ret_gn_g[j], ret_gn_b[j], batch, seq)
            x1, ri, rg, cnt = _ret_tail(y, xt, ret_w_o[j].astype(BF16), ln1_g[i], ln1_b[i], wrh, wrl, br)
        xt = _moe(x1, ri, rg, cnt, moe_w_gate[i].astype(BF16), moe_w_up[i].astype(BF16),
                  moe_w_down[i].astype(BF16), ln2_g[i], ln2_b[i])
    return xt.reshape(batch, seq, d)
```

```python
import functools
import math

import jax
import jax.numpy as jnp
from jax import lax
from jax.experimental import pallas as pl
from jax.experimental.pallas import tpu as pltpu

F32 = jnp.float32
BF16 = jnp.bfloat16
I32 = jnp.int32

DEPTH = 4
N_MIXERS = 2
CONV_WIDTH = 31
RET_HEADS = 4
N_GROUPS = 4
EXPERTS_PER_GROUP = 8
N_EXPERTS = N_GROUPS * EXPERTS_PER_GROUP
ROPE_BASE = 10000.0
DEEPNORM_ALPHA = (2.0 * DEPTH) ** 0.25
LN_EPS = 1e-5

LANES = 128
SUBLANES = 8
VMEM_LIMIT = 56 * 1024 * 1024

HALO = 32
CONV_ROWS = 32
TS_TAIL = 256
TM_PW1 = 512
TM_QKVG = 256
RET_BLOCK = 256
RET_SUPER = 1024
TM_FFN = 512
G_DISPATCH = 512
G_COMBINE = 256
ROUTE_LANE0 = N_GROUPS


def _cparams(sem):
    return pltpu.CompilerParams(dimension_semantics=sem, vmem_limit_bytes=VMEM_LIMIT)


def _ln(x, g, b):
    mu = jnp.mean(x, axis=-1, keepdims=True)
    xc = x - mu
    var = jnp.mean(xc * xc, axis=-1, keepdims=True)
    return xc * lax.rsqrt(var + LN_EPS) * g + b


def _silu(x):
    return x * jax.nn.sigmoid(x)


def _store_rows_tiled(o_ref, val):
    m = val.shape[0]
    for c in range(val.shape[1] // LANES):
        o_ref[pl.ds(c, m, stride=SUBLANES), :] = val[:, c * LANES:(c + 1) * LANES]


def _load_rows_tiled(ref, m, base=0):
    return [ref[pl.ds(base + c, m, stride=SUBLANES), :] for c in range(SUBLANES)]


def _ln1_route(pre, l1g_ref, l1b_ref, wrh_ref, wrl_ref, br_ref, base_ref, x1_ref, ri_ref, rg_ref, cnt_ref):
    ts = pre.shape[0]
    x1 = _ln(pre, l1g_ref[...], l1b_ref[...])
    _store_rows_tiled(x1_ref, x1)

    xh = x1.astype(BF16)
    xl = (x1 - xh.astype(F32)).astype(BF16)
    wrh = wrh_ref[...]
    logits = (jnp.dot(xh, wrh, preferred_element_type=F32)
              + jnp.dot(xl, wrh, preferred_element_type=F32)
              + jnp.dot(xh, wrl_ref[...], preferred_element_type=F32)) + br_ref[...]

    col = lax.broadcasted_iota(I32, logits.shape, 1).astype(F32)
    neg = jnp.float32(-jnp.inf)
    no_lane = jnp.float32(LANES)
    is_grp = col < N_GROUPS
    gl = jnp.where(is_grp, logits, neg)
    gm = jnp.max(gl, axis=-1, keepdims=True)
    gidx = jnp.min(jnp.where(gl == gm, col, no_lane), axis=-1, keepdims=True)
    denom = jnp.sum(jnp.where(is_grp, jnp.exp(gl - gm), 0.0), axis=-1, keepdims=True)
    p_g = 1.0 / denom

    lo = ROUTE_LANE0 + EXPERTS_PER_GROUP * gidx
    sel = (col >= lo) & (col < lo + EXPERTS_PER_GROUP)
    sl = jnp.where(sel, logits, neg)
    m1 = jnp.max(sl, axis=-1, keepdims=True)
    i1 = jnp.min(jnp.where(sl == m1, col, no_lane), axis=-1, keepdims=True)
    sl2 = jnp.where(col == i1, neg, sl)
    m2 = jnp.max(sl2, axis=-1, keepdims=True)
    i2 = jnp.min(jnp.where(sl2 == m2, col, no_lane), axis=-1, keepdims=True)
    e21 = jnp.exp(m2 - m1)
    g0 = p_g / (1.0 + e21)
    g1 = p_g * e21 / (1.0 + e21)

    oh0 = col == i1
    oh1 = col == i2
    s_f = jnp.where(oh0 | oh1, 1.0, 0.0)
    rr = lax.broadcasted_iota(I32, (ts, ts), 0)
    cc = lax.broadcasted_iota(I32, (ts, ts), 1)
    tri = jnp.where(cc < rr, 1.0, 0.0).astype(BF16)
    before = jnp.dot(tri, s_f.astype(BF16), preferred_element_type=F32) + base_ref[...]
    r0 = jnp.sum(jnp.where(oh0, before, 0.0), axis=-1, keepdims=True)
    r1 = jnp.sum(jnp.where(oh1, before, 0.0), axis=-1, keepdims=True)
    base_ref[...] += jnp.sum(s_f, axis=0, keepdims=True)
    cnt_ref[...] = base_ref[...]

    ri = jnp.where(col == 0, i1 - ROUTE_LANE0,
                   jnp.where(col == 1, i2 - ROUTE_LANE0,
                             jnp.where(col == 2, r0, jnp.where(col == 3, r1, 0.0))))
    ri_ref[...] = ri.astype(I32)
    rg_ref[...] = jnp.where(col == 0, g0, jnp.where(col == 1, g1, 0.0))


def _tail_out_shapes(t, d):
    return (jax.ShapeDtypeStruct((t * SUBLANES, LANES), F32),
            jax.ShapeDtypeStruct((t, LANES), I32),
            jax.ShapeDtypeStruct((t, LANES), F32),
            jax.ShapeDtypeStruct((1, LANES), F32))


def _pw1_glu_kernel(x_ref, w_ref, b_ref, o_ref):
    d = o_ref.shape[-1]
    h = jnp.dot(x_ref[...].astype(BF16), w_ref[...], preferred_element_type=F32) + b_ref[...]
    o_ref[...] = h[:, :d] * jax.nn.sigmoid(h[:, d:])


def _pw1_glu(x, w_bf, b):
    t, d = x.shape
    tm = TM_PW1
    return pl.pallas_call(
        _pw1_glu_kernel,
        grid=(t // tm,),
        in_specs=[pl.BlockSpec((tm, d), lambda i: (i, 0)),
                  pl.BlockSpec((d, 2 * d), lambda i: (0, 0)),
                  pl.BlockSpec((1, 2 * d), lambda i: (0, 0))],
        out_specs=pl.BlockSpec((tm, d), lambda i: (i, 0)),
        out_shape=jax.ShapeDtypeStruct((t, d), F32),
        compiler_params=_cparams(("arbitrary",)),
        name="conv_pw1_glu",
    )(x, w_bf, b.reshape(1, -1))


def _conv_tail_kernel(hcur_ref, hprev_ref, x_ref, wdw_ref, bdw_ref, lng_ref, lnb_ref, wpw2_ref, bpw2_ref,
                      l1g_ref, l1b_ref, wrh_ref, wrl_ref, br_ref,
                      x1_ref, ri_ref, rg_ref, cnt_ref,
                      hext_ref, hsh_ref, conv_ref, base_ref):
    ts, d = hcur_ref.shape
    b_id = pl.program_id(0)
    j = pl.program_id(1)

    @pl.when((b_id == 0) & (j == 0))
    def _():
        base_ref[...] = jnp.zeros_like(base_ref)

    hext_ref[HALO:, :] = hcur_ref[...]

    @pl.when(j == 0)
    def _():
        hext_ref[0:HALO, :] = jnp.zeros((HALO, d), F32)

    @pl.when(j > 0)
    def _():
        hext_ref[0:HALO, :] = hprev_ref[...]

    n_sh = ts + HALO - SUBLANES
    for b in range(1, SUBLANES):
        hsh_ref[b - 1] = hext_ref[b:b + n_sh, :]
    offs = [HALO - (CONV_WIDTH - 1) + k for k in range(CONV_WIDTH)]

    for c in range(d // LANES):
        lanes = slice(c * LANES, (c + 1) * LANES)
        w_rows = [jnp.broadcast_to(wdw_ref[k:k + 1, lanes], (SUBLANES, LANES)) for k in range(CONV_WIDTH)]
        bias = jnp.broadcast_to(bdw_ref[:, lanes], (SUBLANES, LANES))

        def conv_rows(r, carry, lanes=lanes, w_rows=w_rows, bias=bias):
            r0 = pl.multiple_of(r * CONV_ROWS, CONV_ROWS)
            for grp in range(CONV_ROWS // SUBLANES):
                acc = bias
                for k, off in enumerate(offs):
                    start = r0 + (grp + off // SUBLANES) * SUBLANES
                    if off % SUBLANES == 0:
                        tap = hext_ref[pl.ds(start, SUBLANES), lanes]
                    else:
                        tap = hsh_ref[off % SUBLANES - 1, pl.ds(start, SUBLANES), lanes]
                    acc = acc + w_rows[k] * tap
                conv_ref[pl.ds(r0 + grp * SUBLANES, SUBLANES), lanes] = acc
            return carry

        lax.fori_loop(0, ts // CONV_ROWS, conv_rows, 0)

    hn = _silu(_ln(conv_ref[...], lng_ref[...], lnb_ref[...]))
    mix = jnp.dot(hn.astype(BF16), wpw2_ref[...], preferred_element_type=F32) + bpw2_ref[...]
    pre = DEEPNORM_ALPHA * x_ref[...] + mix
    _ln1_route(pre, l1g_ref, l1b_ref, wrh_ref, wrl_ref, br_ref, base_ref, x1_ref, ri_ref, rg_ref, cnt_ref)


def _conv_tail(h, x, batch, seq, w_dw, b_dw, ln_g, ln_b, w_pw2_bf, b_pw2, l1g, l1b, wrh, wrl, br):
    t, d = x.shape
    ts = TS_TAIL
    nj = seq // ts
    halo_per_tile = ts // HALO
    h3 = h.reshape(batch, seq, d)
    x3 = x.reshape(batch, seq, d)
    row = lambda v: v.reshape(1, -1)
    const2 = lambda shape: pl.BlockSpec(shape, lambda b, j: (0, 0))
    tok = lambda b, j: (b * nj + j, 0)
    return pl.pallas_call(
        _conv_tail_kernel,
        grid=(batch, nj),
        in_specs=[pl.BlockSpec((None, ts, d), lambda b, j: (b, j, 0)),
                  pl.BlockSpec((None, HALO, d), lambda b, j: (b, jnp.maximum(j * halo_per_tile - 1, 0), 0)),
                  pl.BlockSpec((None, ts, d), lambda b, j: (b, j, 0)),
                  const2((CONV_WIDTH, d)), const2((1, d)), const2((1, d)), const2((1, d)),
                  const2((d, d)), const2((1, d)), const2((1, d)), const2((1, d)),
                  const2((d, LANES)), const2((d, LANES)), const2((1, LANES))],
        out_specs=(pl.BlockSpec((ts * SUBLANES, LANES), tok),
                   pl.BlockSpec((ts, LANES), tok),
                   pl.BlockSpec((ts, LANES), tok),
                   pl.BlockSpec((1, LANES), lambda b, j: (0, 0))),
        out_shape=_tail_out_shapes(t, d),
        scratch_shapes=[pltpu.VMEM((ts + HALO, d), F32), pltpu.VMEM((SUBLANES - 1, ts + HALO - SUBLANES, d), F32),
                        pltpu.VMEM((ts, d), F32), pltpu.VMEM((1, LANES), F32)],
        compiler_params=_cparams(("arbitrary", "arbitrary")),
        name="conv_tail_ln1_route",
    )(h3, h3, x3, w_dw, row(b_dw), row(ln_g), row(ln_b), w_pw2_bf, row(b_pw2), row(l1g), row(l1b), wrh, wrl, br)


def _rope_table_kernel(pos_ref, invf_ref, cos_ref, sin_ref):
    ang = pos_ref[...].astype(F32) * invf_ref[...]
    cos_ref[...] = jnp.cos(ang)
    sin_ref[...] = jnp.sin(ang)


def _rope_tables(positions, half):
    t = positions.size
    tm = 1024
    inv_freq = ROPE_BASE ** (-jnp.arange(half, dtype=F32) / half)
    return pl.pallas_call(
        _rope_table_kernel,
        grid=(t // tm,),
        in_specs=[pl.BlockSpec((tm, 1), lambda i: (i, 0)), pl.BlockSpec((1, half), lambda i: (0, 0))],
        out_specs=(pl.BlockSpec((tm, half), lambda i: (i, 0)), pl.BlockSpec((tm, half), lambda i: (i, 0))),
        out_shape=(jax.ShapeDtypeStruct((t, half), F32), jax.ShapeDtypeStruct((t, half), F32)),
        compiler_params=_cparams(("arbitrary",)),
        name="rope_tables",
    )(positions.reshape(t, 1), inv_freq.reshape(1, half))


def _rotate(t, cos, sin, head_dim):
    half = head_dim // 2
    parts = []
    for h in range(t.shape[1] // head_dim):
        t1 = t[:, h * head_dim:h * head_dim + half]
        t2 = t[:, h * head_dim + half:(h + 1) * head_dim]
        parts.append(t1 * cos - t2 * sin)
        parts.append(t1 * sin + t2 * cos)
    return jnp.concatenate(parts, axis=-1)


def _qkvg_kernel(x_ref, wq_ref, wk_ref, wv_ref, wg_ref, cos_ref, sin_ref, q_ref, k_ref, v_ref, g_ref, *, head_qk):
    xb = x_ref[...].astype(BF16)
    cos = cos_ref[...]
    sin = sin_ref[...]
    q = jnp.dot(xb, wq_ref[...], preferred_element_type=F32)
    q_ref[...] = _rotate(q, cos, sin, head_qk).astype(BF16)
    k = jnp.dot(xb, wk_ref[...], preferred_element_type=F32)
    k_ref[...] = (_rotate(k, cos, sin, head_qk) * (head_qk ** -0.5)).astype(BF16)
    v_ref[...] = jnp.dot(xb, wv_ref[...], preferred_element_type=F32).astype(BF16)
    g_ref[...] = jnp.dot(xb, wg_ref[...], preferred_element_type=F32)


def _qkvg(x, wq, wk, wv, wg, cos, sin):
    t, d = x.shape
    qk = wq.shape[1]
    vd = wv.shape[1]
    head_qk = qk // RET_HEADS
    tm = TM_QKVG
    tokb = lambda n: pl.BlockSpec((tm, n), lambda i: (i, 0))
    wspec = lambda n: pl.BlockSpec((d, n), lambda i: (0, 0))
    return pl.pallas_call(
        functools.partial(_qkvg_kernel, head_qk=head_qk),
        grid=(t // tm,),
        in_specs=[tokb(d), wspec(qk), wspec(qk), wspec(vd), wspec(vd), tokb(head_qk // 2), tokb(head_qk // 2)],
        out_specs=(tokb(qk), tokb(qk), tokb(vd), tokb(vd)),
        out_shape=(jax.ShapeDtypeStruct((t, qk), BF16), jax.ShapeDtypeStruct((t, qk), BF16),
                   jax.ShapeDtypeStruct((t, vd), BF16), jax.ShapeDtypeStruct((t, vd), F32)),
        compiler_params=_cparams(("arbitrary",)),
        name="ret_qkvg_rope",
    )(x, wq, wk, wv, wg, cos, sin)


def _ret_core_kernel(q_ref, k_ref, v_ref, g_ref, mask_ref, xi_ref, zeta_ref, dec_ref, gng_ref, gnb_ref,
                     y_ref, state_ref):
    blk = RET_BLOCK
    n_blk = q_ref.shape[0] // blk

    @pl.when(pl.program_id(2) == 0)
    def _():
        state_ref[...] = jnp.zeros_like(state_ref)

    mask = mask_ref[...]
    xi = xi_ref[...]
    zeta = zeta_ref[...]
    dec = dec_ref[0:1, 0:1]
    for n in range(n_blk):
        rows = slice(n * blk, (n + 1) * blk)
        q = q_ref[rows, :]
        k = k_ref[rows, :]
        v = v_ref[rows, :]
        state = state_ref[...]
        s = lax.dot_general(q, k, (((1,), (1,)), ((), ())), preferred_element_type=F32)
        p = (s * mask).astype(BF16)
        o = jnp.dot(p, v, preferred_element_type=F32)
        qx = (q.astype(F32) * xi).astype(BF16)
        o = o + jnp.dot(qx, state.astype(BF16), preferred_element_type=F32)
        kz = (k.astype(F32) * zeta).astype(BF16)
        state_ref[...] = state * dec + lax.dot_general(kz, v, (((0,), (0,)), ((), ())),
                                                       preferred_element_type=F32)
        mu = jnp.mean(o, axis=-1, keepdims=True)
        oc = o - mu
        var = jnp.mean(oc * oc, axis=-1, keepdims=True)
        on = oc * lax.rsqrt(var + LN_EPS) * gng_ref[...] + gnb_ref[...]
        y_ref[rows, :] = (_silu(g_ref[rows, :]) * on).astype(BF16)


def _ret_tables(head_qk):
    del head_qk
    blk = RET_BLOCK
    chunk = 64
    log_gamma = jnp.log(1.0 - 2.0 ** (-5.0 - jnp.arange(RET_HEADS, dtype=F32)))
    idx = jnp.arange(blk, dtype=F32)
    dist = jnp.abs(idx[:, None] - idx[None, :])
    visible = (jnp.floor(idx[None, :] / chunk) <= jnp.floor(idx[:, None] / chunk))
    mask = jnp.where(visible[None], jnp.exp(log_gamma[:, None, None] * dist[None]), 0.0)
    xi = jnp.exp(log_gamma[:, None] * (idx[None, :] + 1.0))[..., None]
    zeta = jnp.exp(log_gamma[:, None] * (blk - 1.0 - idx[None, :]))[..., None]
    dec = jnp.broadcast_to(jnp.exp(log_gamma * blk)[:, None, None], (RET_HEADS, SUBLANES, LANES))
    return mask.astype(F32), xi.astype(F32), zeta.astype(F32), dec.astype(F32)


def _ret_core(q, k, v, g, gn_g, gn_b, batch, seq):
    t, qk = q.shape
    vd = v.shape[1]
    hq = qk // RET_HEADS
    hv = vd // RET_HEADS
    sb = RET_SUPER
    ns = seq // sb
    mask, xi, zeta, dec = _ret_tables(hq)
    tokb = lambda n: pl.BlockSpec((sb, n), lambda b, h, s: (b * ns + s, h))
    headb = lambda r, c: pl.BlockSpec((None, r, c), lambda b, h, s: (h, 0, 0))
    return pl.pallas_call(
        _ret_core_kernel,
        grid=(batch, RET_HEADS, ns),
        in_specs=[tokb(hq), tokb(hq), tokb(hv), tokb(hv),
                  headb(RET_BLOCK, RET_BLOCK), headb(RET_BLOCK, 1), headb(RET_BLOCK, 1), headb(SUBLANES, LANES),
                  pl.BlockSpec((1, hv), lambda b, h, s: (0, h)), pl.BlockSpec((1, hv), lambda b, h, s: (0, h))],
        out_specs=tokb(hv),
        out_shape=jax.ShapeDtypeStruct((t, vd), BF16),
        scratch_shapes=[pltpu.VMEM((hq, hv), F32)],
        compiler_params=_cparams(("arbitrary", "arbitrary", "arbitrary")),
        name="ret_core",
    )(q, k, v, g, mask, xi, zeta, dec, gn_g.reshape(1, -1), gn_b.reshape(1, -1))


def _ret_tail_kernel(y_ref, x_ref, wo_ref, l1g_ref, l1b_ref, wrh_ref, wrl_ref, br_ref,
                     x1_ref, ri_ref, rg_ref, cnt_ref, base_ref):
    @pl.when(pl.program_id(0) == 0)
    def _():
        base_ref[...] = jnp.zeros_like(base_ref)

    mix = jnp.dot(y_ref[...], wo_ref[...], preferred_element_type=F32)
    pre = DEEPNORM_ALPHA * x_ref[...] + mix
    _ln1_route(pre, l1g_ref, l1b_ref, wrh_ref, wrl_ref, br_ref, base_ref, x1_ref, ri_ref, rg_ref, cnt_ref)


def _ret_tail(y, x, w_o_bf, l1g, l1b, wrh, wrl, br):
    t, d = x.shape
    vd = y.shape[1]
    ts = TS_TAIL
    row = lambda v: v.reshape(1, -1)
    const = lambda shape: pl.BlockSpec(shape, lambda i: (0, 0))
    tok = lambda i: (i, 0)
    return pl.pallas_call(
        _ret_tail_kernel,
        grid=(t // ts,),
        in_specs=[pl.BlockSpec((ts, vd), tok), pl.BlockSpec((ts, d), tok), const((vd, d)),
                  const((1, d)), const((1, d)), const((d, LANES)), const((d, LANES)), const((1, LANES))],
        out_specs=(pl.BlockSpec((ts * SUBLANES, LANES), tok),
                   pl.BlockSpec((ts, LANES), tok),
                   pl.BlockSpec((ts, LANES), tok),
                   pl.BlockSpec((1, LANES), lambda i: (0, 0))),
        out_shape=_tail_out_shapes(t, d),
        scratch_shapes=[pltpu.VMEM((1, LANES), F32)],
        compiler_params=_cparams(("arbitrary",)),
        name="ret_tail_ln1_route",
    )(y, x, w_o_bf, row(l1g), row(l1b), wrh, wrl, br)


def _dispatch_kernel(pos_ref, x1_ref, xs_init_hbm, xs_hbm, sem):
    del xs_init_hbm
    g = pos_ref.shape[-1] // 2

    def row(ref, r):
        return ref.at[pl.ds(pl.multiple_of(r * SUBLANES, SUBLANES), SUBLANES)]

    def issue(j, carry):
        src = row(x1_ref, j)
        pltpu.make_async_copy(src, row(xs_hbm, pos_ref[0, 0, j]), sem.at[0]).start()
        pltpu.make_async_copy(src, row(xs_hbm, pos_ref[0, 0, g + j]), sem.at[0]).start()
        return carry

    lax.fori_loop(0, g, issue, 0, unroll=8)
    for _ in range(2):
        pltpu.make_async_copy(x1_ref, xs_hbm.at[pl.ds(0, g * SUBLANES)], sem.at[0]).wait()


def _dispatch(x1_2d, pos, n_rows):
    t = x1_2d.shape[0] // SUBLANES
    g = G_DISPATCH
    pos_blocks = pos.reshape(t // g, g, 2).transpose(0, 2, 1).reshape(t // g, 1, 2 * g)
    xs_init = jnp.zeros((n_rows * SUBLANES, LANES), F32)
    return pl.pallas_call(
        _dispatch_kernel,
        grid=(t // g,),
        in_specs=[pl.BlockSpec((1, 1, 2 * g), lambda i: (i, 0, 0), memory_space=pltpu.SMEM),
                  pl.BlockSpec((g * SUBLANES, LANES), lambda i: (i, 0)),
                  pl.BlockSpec(memory_space=pl.ANY)],
        out_specs=pl.BlockSpec(memory_space=pl.ANY),
        out_shape=jax.ShapeDtypeStruct((n_rows * SUBLANES, LANES), F32),
        scratch_shapes=[pltpu.SemaphoreType.DMA((1,))],
        input_output_aliases={2: 0},
        compiler_params=_cparams(("arbitrary",)),
        name="moe_dispatch",
    )(pos_blocks, x1_2d, xs_init)


def _ffn_kernel(tile_e_ref, nt_ref, x_ref, wg_ref, wu_ref, wd_ref, o_ref):
    del tile_e_ref
    tm = o_ref.shape[0] // SUBLANES
    i = pl.program_id(0)

    @pl.when(i < nt_ref[0])
    def _():
        x = jnp.concatenate([c.astype(BF16) for c in _load_rows_tiled(x_ref, tm)], axis=-1)
        a = jnp.dot(x, wg_ref[...], preferred_element_type=F32)
        u = jnp.dot(x, wu_ref[...], preferred_element_type=F32)
        h = (_silu(a) * u).astype(BF16)
        _store_rows_tiled(o_ref, jnp.dot(h, wd_ref[...], preferred_element_type=F32))

    @pl.when(i >= nt_ref[0])
    def _():
        o_ref[...] = jnp.zeros_like(o_ref)


def _ffn(xs, tile_e, n_tiles, wg_bf, wu_bf, wd_bf):
    n_rows = xs.shape[0] // SUBLANES
    tm = TM_FFN
    nt_max = n_rows // tm
    _, d, f = wg_bf.shape
    grid_spec = pltpu.PrefetchScalarGridSpec(
        num_scalar_prefetch=2,
        grid=(nt_max,),
        in_specs=[pl.BlockSpec((tm * SUBLANES, LANES), lambda i, te, nt: (jnp.minimum(i, nt[0] - 1), 0)),
                  pl.BlockSpec((None, d, f), lambda i, te, nt: (te[i], 0, 0)),
                  pl.BlockSpec((None, d, f), lambda i, te, nt: (te[i], 0, 0)),
                  pl.BlockSpec((None, f, d), lambda i, te, nt: (te[i], 0, 0))],
        out_specs=pl.BlockSpec((tm * SUBLANES, LANES), lambda i, te, nt: (i, 0)),
    )
    return pl.pallas_call(
        _ffn_kernel,
        grid_spec=grid_spec,
        out_shape=jax.ShapeDtypeStruct((n_rows * SUBLANES, LANES), F32),
        compiler_params=_cparams(("arbitrary",)),
        name="moe_expert_ffn",
    )(tile_e, n_tiles, xs, wg_bf, wu_bf, wd_bf)


def _combine_ln2_kernel(pos_ref, x1_ref, rg_ref, l2g_ref, l2b_ref, ys_hbm, o_ref, ybuf_ref, sem):
    g = o_ref.shape[0]

    def row(ref, r):
        return ref.at[pl.ds(pl.multiple_of(r * SUBLANES, SUBLANES), SUBLANES)]

    def issue(j, carry):
        pltpu.make_async_copy(row(ys_hbm, pos_ref[0, 0, j]), row(ybuf_ref, j), sem.at[0]).start()
        pltpu.make_async_copy(row(ys_hbm, pos_ref[0, 0, g + j]), row(ybuf_ref, g + j), sem.at[0]).start()
        return carry

    lax.fori_loop(0, g, issue, 0, unroll=8)
    pltpu.make_async_copy(ys_hbm.at[pl.ds(0, 2 * g * SUBLANES)], ybuf_ref, sem.at[0]).wait()

    gates = rg_ref[...]
    g0 = gates[:, 0:1]
    g1 = gates[:, 1:2]
    x1c = _load_rows_tiled(x1_ref, g)
    y0c = _load_rows_tiled(ybuf_ref, g)
    y1c = _load_rows_tiled(ybuf_ref, g, base=g * SUBLANES)
    pre = jnp.concatenate([DEEPNORM_ALPHA * a + (g0 * b + g1 * c) for a, b, c in zip(x1c, y0c, y1c)], axis=-1)
    o_ref[...] = _ln(pre, l2g_ref[...], l2b_ref[...])


def _combine_ln2(x1_2d, rg, pos, ys_tiled, l2g, l2b):
    t = rg.shape[0]
    d = l2g.shape[0]
    g = G_COMBINE
    pos_blocks = pos.reshape(t // g, g, 2).transpose(0, 2, 1).reshape(t // g, 1, 2 * g)
    return pl.pallas_call(
        _combine_ln2_kernel,
        grid=(t // g,),
        in_specs=[pl.BlockSpec((1, 1, 2 * g), lambda i: (i, 0, 0), memory_space=pltpu.SMEM),
                  pl.BlockSpec((g * SUBLANES, LANES), lambda i: (i, 0)),
                  pl.BlockSpec((g, LANES), lambda i: (i, 0)),
                  pl.BlockSpec((1, d), lambda i: (0, 0)),
                  pl.BlockSpec((1, d), lambda i: (0, 0)),
                  pl.BlockSpec(memory_space=pl.ANY)],
        out_specs=pl.BlockSpec((g, d), lambda i: (i, 0)),
        out_shape=jax.ShapeDtypeStruct((t, d), F32),
        scratch_shapes=[pltpu.VMEM((2 * g * SUBLANES, LANES), F32), pltpu.SemaphoreType.DMA((1,))],
        compiler_params=_cparams(("arbitrary",)),
        name="moe_combine_ln2",
    )(pos_blocks, x1_2d, rg, l2g.reshape(1, -1), l2b.reshape(1, -1), ys_tiled)


def _moe(x1_2d, ri, rg, cnt, wg_bf, wu_bf, wd_bf, l2g, l2b):
    t = ri.shape[0]
    tm = TM_FFN
    n_rows = ((2 * t + N_EXPERTS * (tm - 1) + tm - 1) // tm) * tm
    counts = cnt[0, ROUTE_LANE0:ROUTE_LANE0 + N_EXPERTS].astype(I32)
    padded = ((counts + tm - 1) // tm) * tm
    ends = jnp.cumsum(padded)
    offs = ends - padded
    pos = jnp.take(offs, ri[:, 0:2]) + ri[:, 2:4]
    tile_start = jnp.arange(n_rows // tm, dtype=I32) * tm
    tile_e = jnp.minimum(jnp.sum(ends[None, :] <= tile_start[:, None], axis=1), N_EXPERTS - 1).astype(I32)
    n_tiles = (ends[-1:] // tm).astype(I32)

    xs = _dispatch(x1_2d, pos, n_rows)
    ys = _ffn(xs, tile_e, n_tiles, wg_bf, wu_bf, wd_bf)
    return _combine_ln2(x1_2d, rg, pos, ys, l2g, l2b)


def _router_weights(w_grp, b_grp, w_route, b_route):
    d = w_grp.shape[0]
    used = N_GROUPS + N_EXPERTS
    w = jnp.concatenate([w_grp, w_route, jnp.zeros((d, LANES - used), F32)], axis=1)
    b = jnp.concatenate([b_grp, b_route, jnp.zeros((LANES - used,), F32)]).reshape(1, LANES)
    wh = w.astype(BF16)
    wl = (w - wh.astype(F32)).astype(BF16)
    return wh, wl, b


def kernel(x, positions, conv_w_pw1, conv_b_pw1, conv_w_dw, conv_b_dw, conv_ln_g, conv_ln_b, conv_w_pw2, conv_b_pw2,
           ret_w_qkvg, ret_gn_g, ret_gn_b, ret_w_o, ln1_g, ln1_b, ln2_g, ln2_b, moe_w_grp, moe_b_grp, moe_w_route,
           moe_b_route, moe_w_gate, moe_w_up, moe_w_down):
    batch, seq, d = x.shape
    t = batch * seq
    qk = d
    vd = 2 * d
    xt = x.reshape(t, d)
    cos, sin = _rope_tables(positions, qk // RET_HEADS // 2)
    for i in range(DEPTH):
        j = i // N_MIXERS
        wrh, wrl, br = _router_weights(moe_w_grp[i], moe_b_grp[i], moe_w_route[i], moe_b_route[i])
        if i % N_MIXERS == 0:
            h = _pw1_glu(xt, conv_w_pw1[j].astype(BF16), conv_b_pw1[j])
            x1, ri, rg, cnt = _conv_tail(h, xt, batch, seq, conv_w_dw[j], conv_b_dw[j], conv_ln_g[j], conv_ln_b[j],
                                         conv_w_pw2[j].astype(BF16), conv_b_pw2[j], ln1_g[i], ln1_b[i], wrh, wrl, br)
        else:
            w = ret_w_qkvg[j].astype(BF16)
            q, k, v, g = _qkvg(xt, w[:, :qk], w[:, qk:2 * qk], w[:, 2 * qk:2 * qk + vd], w[:, 2 * qk + vd:], cos, sin)
            y = _ret_core(q, k, v, g, ret_gn_g[j], ret_gn_b[j], batch, seq)
            x1, ri, rg, cnt = _ret_tail(y, xt, ret_w_o[j].astype(BF16), ln1_g[i], ln1_b[i], wrh, wrl, br)
        xt = _moe(x1, ri, rg, cnt, moe_w_gate[i].astype(BF16), moe_w_up[i].astype(BF16),
                  moe_w_down[i].astype(BF16), ln2_g[i], ln2_b[i])
    return xt.reshape(batch, seq, d)
```

```python
import functools

import jax
import jax.numpy as jnp
from jax import lax
from jax.experimental import pallas as pl
from jax.experimental.pallas import tpu as pltpu

F32 = jnp.float32
BF16 = jnp.bfloat16
I32 = jnp.int32

DEPTH = 4
N_MIXERS = 2
CONV_WIDTH = 31
RET_HEADS = 4
RET_CHUNK = 64
N_GROUPS = 4
EXPERTS_PER_GROUP = 8
N_EXPERTS = N_GROUPS * EXPERTS_PER_GROUP
TOP_K = 2
ROPE_BASE = 10000.0
DEEPNORM_ALPHA = (2.0 * DEPTH) ** 0.25
LN_EPS = 1e-5

LANES = 128
SUBLANES = 8
VMEM_LIMIT = 56 * 1024 * 1024

HALO = 32
CONV_ROWS = 32
TS_TAIL = 256
TAIL_WINDOWS = 2
TS_STEP = TAIL_WINDOWS * TS_TAIL
TM_PW1 = 512
TM_QKVG = 256
RET_BLOCK = 256
RET_SUPER = 1024
TM_FFN = 512
ROUTE_LANE0 = N_GROUPS
WIN_ROWS = -(-(TOP_K * TS_TAIL + N_EXPERTS * (SUBLANES - 1) + SUBLANES) // LANES) * LANES
WIN_GROUPS = WIN_ROWS // SUBLANES
TILE_GROUPS = TM_FFN // SUBLANES


def _cparams(sem):
    return pltpu.CompilerParams(dimension_semantics=sem, vmem_limit_bytes=VMEM_LIMIT)


def _ln(x, g, b):
    mu = jnp.mean(x, axis=-1, keepdims=True)
    xc = x - mu
    var = jnp.mean(xc * xc, axis=-1, keepdims=True)
    return xc * lax.rsqrt(var + LN_EPS) * g + b


def _silu(x):
    return x * jax.nn.sigmoid(x)


def _row_group(ref, row):
    return ref.at[pl.ds(pl.multiple_of(row, SUBLANES), SUBLANES)]


def _ln1_route_sort(pre, l1g_ref, l1b_ref, wrh_ref, wrl_ref, br_ref, x1_ref, xs_ref, rg_ref, cw_ref):
    for w in range(pre.shape[0] // TS_TAIL):
        rows = slice(w * TS_TAIL, (w + 1) * TS_TAIL)
        _ln1_route_sort_window(pre[rows, :], l1g_ref, l1b_ref, wrh_ref, wrl_ref, br_ref, x1_ref.at[rows, :],
                               xs_ref.at[w * WIN_ROWS:(w + 1) * WIN_ROWS, :], rg_ref.at[rows, :],
                               cw_ref.at[w * SUBLANES:(w + 1) * SUBLANES, :])


def _ln1_route_sort_window(pre, l1g_ref, l1b_ref, wrh_ref, wrl_ref, br_ref, x1_ref, xs_ref, rg_ref, cw_ref):
    ts = pre.shape[0]
    x1 = _ln(pre, l1g_ref[...], l1b_ref[...])
    x1_ref[...] = x1

    xh = x1.astype(BF16)
    xl = (x1 - xh.astype(F32)).astype(BF16)
    wrh = wrh_ref[...]
    logits = (jnp.dot(xh, wrh, preferred_element_type=F32)
              + jnp.dot(xl, wrh, preferred_element_type=F32)
              + jnp.dot(xh, wrl_ref[...], preferred_element_type=F32)) + br_ref[...]

    col = lax.broadcasted_iota(I32, logits.shape, 1).astype(F32)
    neg = jnp.float32(-jnp.inf)
    no_lane = jnp.float32(LANES)
    is_grp = col < N_GROUPS
    gl = jnp.where(is_grp, logits, neg)
    gm = jnp.max(gl, axis=-1, keepdims=True)
    gidx = jnp.min(jnp.where(gl == gm, col, no_lane), axis=-1, keepdims=True)
    denom = jnp.sum(jnp.where(is_grp, jnp.exp(gl - gm), 0.0), axis=-1, keepdims=True)
    p_g = 1.0 / denom

    lo = ROUTE_LANE0 + EXPERTS_PER_GROUP * gidx
    sel = (col >= lo) & (col < lo + EXPERTS_PER_GROUP)
    sl = jnp.where(sel, logits, neg)
    m1 = jnp.max(sl, axis=-1, keepdims=True)
    i1 = jnp.min(jnp.where(sl == m1, col, no_lane), axis=-1, keepdims=True)
    sl2 = jnp.where(col == i1, neg, sl)
    m2 = jnp.max(sl2, axis=-1, keepdims=True)
    i2 = jnp.min(jnp.where(sl2 == m2, col, no_lane), axis=-1, keepdims=True)
    e21 = jnp.exp(m2 - m1)
    g0 = p_g / (1.0 + e21)
    g1 = p_g * e21 / (1.0 + e21)

    oh0 = col == i1
    oh1 = col == i2
    s_f = jnp.where(oh0 | oh1, 1.0, 0.0)
    rr = lax.broadcasted_iota(I32, (ts, ts), 0)
    cc = lax.broadcasted_iota(I32, (ts, ts), 1)
    earlier = jnp.where(cc < rr, 1.0, 0.0).astype(BF16)
    rank = jnp.dot(earlier, s_f.astype(BF16), preferred_element_type=F32)
    cnt = jnp.sum(s_f, axis=0, keepdims=True)
    run = jnp.ceil(cnt * (1.0 / SUBLANES)) * SUBLANES
    er = lax.broadcasted_iota(I32, (LANES, LANES), 0)
    ec = lax.broadcasted_iota(I32, (LANES, LANES), 1)
    lower = jnp.where(er < ec, 1.0, 0.0).astype(BF16)
    run_start = jnp.dot(jnp.broadcast_to(run, (SUBLANES, LANES)).astype(BF16), lower,
                        preferred_element_type=F32)[0:1, :]
    where_to = rank + run_start
    lp0 = jnp.sum(jnp.where(oh0, where_to, 0.0), axis=-1, keepdims=True)
    lp1 = jnp.sum(jnp.where(oh1, where_to, 0.0), axis=-1, keepdims=True)

    srow = lax.broadcasted_iota(I32, (ts, WIN_ROWS), 1).astype(F32)
    place = jnp.where((srow == lp0) | (srow == lp1), 1.0, 0.0).astype(BF16)
    xs_ref[...] = lax.dot_general(place, xh, (((0,), (0,)), ((), ())), preferred_element_type=F32)

    rg_ref[...] = jnp.where(col == 0, g0, jnp.where(col == 1, g1,
                            jnp.where(col == 2, lp0, jnp.where(col == 3, lp1, 0.0))))
    cw_ref[...] = jnp.broadcast_to(cnt, cw_ref.shape)


def _tail_out_shapes(t, d):
    nw = t // TS_TAIL
    return (jax.ShapeDtypeStruct((t, d), F32),
            jax.ShapeDtypeStruct((nw * WIN_ROWS, d), F32),
            jax.ShapeDtypeStruct((t, LANES), F32),
            jax.ShapeDtypeStruct((nw * SUBLANES, LANES), F32))


def _tail_out_specs(d, tok):
    return (pl.BlockSpec((TS_STEP, d), tok),
            pl.BlockSpec((TAIL_WINDOWS * WIN_ROWS, d), tok),
            pl.BlockSpec((TS_STEP, LANES), tok),
            pl.BlockSpec((TAIL_WINDOWS * SUBLANES, LANES), tok))


def _pw1_glu_kernel(x_ref, w_ref, b_ref, o_ref):
    d = o_ref.shape[-1]
    h = jnp.dot(x_ref[...].astype(BF16), w_ref[...], preferred_element_type=F32) + b_ref[...]
    o_ref[...] = h[:, :d] * jax.nn.sigmoid(h[:, d:])


def _pw1_glu(x, w_bf, b):
    t, d = x.shape
    tm = TM_PW1
    return pl.pallas_call(
        _pw1_glu_kernel,
        grid=(t // tm,),
        in_specs=[pl.BlockSpec((tm, d), lambda i: (i, 0)),
                  pl.BlockSpec((d, 2 * d), lambda i: (0, 0)),
                  pl.BlockSpec((1, 2 * d), lambda i: (0, 0))],
        out_specs=pl.BlockSpec((tm, d), lambda i: (i, 0)),
        out_shape=jax.ShapeDtypeStruct((t, d), F32),
        compiler_params=_cparams(("arbitrary",)),
        name="conv_pw1_glu",
    )(x, w_bf, b.reshape(1, -1))


def _conv_tail_kernel(hcur_ref, hprev_ref, x_ref, wdw_ref, bdw_ref, lng_ref, lnb_ref, wpw2_ref, bpw2_ref,
                      l1g_ref, l1b_ref, wrh_ref, wrl_ref, br_ref,
                      x1_ref, xs_ref, rg_ref, cw_ref,
                      hext_ref, hsh_ref, conv_ref):
    ts, d = hcur_ref.shape
    j = pl.program_id(1)

    hext_ref[HALO:, :] = hcur_ref[...]

    @pl.when(j == 0)
    def _():
        hext_ref[0:HALO, :] = jnp.zeros((HALO, d), F32)

    @pl.when(j > 0)
    def _():
        hext_ref[0:HALO, :] = hprev_ref[...]

    n_sh = ts + HALO - SUBLANES
    offs = [HALO - (CONV_WIDTH - 1) + k for k in range(CONV_WIDTH)]

    for c in range(d // LANES):
        lanes = slice(c * LANES, (c + 1) * LANES)
        for b in range(1, SUBLANES):
            hsh_ref[b - 1] = hext_ref[b:b + n_sh, lanes]
        w_rows = [jnp.broadcast_to(wdw_ref[k:k + 1, lanes], (SUBLANES, LANES)) for k in range(CONV_WIDTH)]
        bias = jnp.broadcast_to(bdw_ref[:, lanes], (SUBLANES, LANES))

        def conv_rows(r, carry, lanes=lanes, w_rows=w_rows, bias=bias):
            r0 = pl.multiple_of(r * CONV_ROWS, CONV_ROWS)
            for grp in range(CONV_ROWS // SUBLANES):
                acc = bias
                for k, off in enumerate(offs):
                    start = r0 + (grp + off // SUBLANES) * SUBLANES
                    if off % SUBLANES == 0:
                        tap = hext_ref[pl.ds(start, SUBLANES), lanes]
                    else:
                        tap = hsh_ref[off % SUBLANES - 1, pl.ds(start, SUBLANES), :]
                    acc = acc + w_rows[k] * tap
                conv_ref[pl.ds(r0 + grp * SUBLANES, SUBLANES), lanes] = acc
            return carry

        lax.fori_loop(0, ts // CONV_ROWS, conv_rows, 0)

    hn = _silu(_ln(conv_ref[...], lng_ref[...], lnb_ref[...]))
    mix = jnp.dot(hn.astype(BF16), wpw2_ref[...], preferred_element_type=F32) + bpw2_ref[...]
    pre = DEEPNORM_ALPHA * x_ref[...] + mix
    _ln1_route_sort(pre, l1g_ref, l1b_ref, wrh_ref, wrl_ref, br_ref, x1_ref, xs_ref, rg_ref, cw_ref)


def _conv_tail(h, x, batch, seq, w_dw, b_dw, ln_g, ln_b, w_pw2_bf, b_pw2, l1g, l1b, wrh, wrl, br):
    t, d = x.shape
    ts = TS_STEP
    nj = seq // ts
    halo_per_tile = ts // HALO
    h3 = h.reshape(batch, seq, d)
    x3 = x.reshape(batch, seq, d)
    row = lambda v: v.reshape(1, -1)
    const2 = lambda shape: pl.BlockSpec(shape, lambda b, j: (0, 0))
    tok = lambda b, j: (b * nj + j, 0)
    return pl.pallas_call(
        _conv_tail_kernel,
        grid=(batch, nj),
        in_specs=[pl.BlockSpec((None, ts, d), lambda b, j: (b, j, 0)),
                  pl.BlockSpec((None, HALO, d), lambda b, j: (b, jnp.maximum(j * halo_per_tile - 1, 0), 0)),
                  pl.BlockSpec((None, ts, d), lambda b, j: (b, j, 0)),
                  const2((CONV_WIDTH, d)), const2((1, d)), const2((1, d)), const2((1, d)),
                  const2((d, d)), const2((1, d)), const2((1, d)), const2((1, d)),
                  const2((d, LANES)), const2((d, LANES)), const2((1, LANES))],
        out_specs=_tail_out_specs(d, tok),
        out_shape=_tail_out_shapes(t, d),
        scratch_shapes=[pltpu.VMEM((ts + HALO, d), F32), pltpu.VMEM((SUBLANES - 1, ts + HALO - SUBLANES, LANES), F32),
                        pltpu.VMEM((ts, d), F32)],
        compiler_params=_cparams(("arbitrary", "arbitrary")),
        name="conv_tail_ln1_route",
    )(h3, h3, x3, w_dw, row(b_dw), row(ln_g), row(ln_b), w_pw2_bf, row(b_pw2), row(l1g), row(l1b), wrh, wrl, br)


def _rope_table_kernel(pos_ref, invf_ref, cos_ref, sin_ref):
    ang = pos_ref[...].astype(F32) * invf_ref[...]
    cos_ref[...] = jnp.cos(ang)
    sin_ref[...] = jnp.sin(ang)


def _rope_tables(positions, half):
    t = positions.size
    tm = 1024
    inv_freq = ROPE_BASE ** (-jnp.arange(half, dtype=F32) / half)
    return pl.pallas_call(
        _rope_table_kernel,
        grid=(t // tm,),
        in_specs=[pl.BlockSpec((tm, 1), lambda i: (i, 0)), pl.BlockSpec((1, half), lambda i: (0, 0))],
        out_specs=(pl.BlockSpec((tm, half), lambda i: (i, 0)), pl.BlockSpec((tm, half), lambda i: (i, 0))),
        out_shape=(jax.ShapeDtypeStruct((t, half), F32), jax.ShapeDtypeStruct((t, half), F32)),
        compiler_params=_cparams(("arbitrary",)),
        name="rope_tables",
    )(positions.reshape(t, 1), inv_freq.reshape(1, half))


def _rotate(t, cos, sin, head_dim):
    half = head_dim // 2
    parts = []
    for h in range(t.shape[1] // head_dim):
        t1 = t[:, h * head_dim:h * head_dim + half]
        t2 = t[:, h * head_dim + half:(h + 1) * head_dim]
        parts.append(t1 * cos - t2 * sin)
        parts.append(t1 * sin + t2 * cos)
    return jnp.concatenate(parts, axis=-1)


def _qkvg_kernel(x_ref, wq_ref, wk_ref, wv_ref, wg_ref, cos_ref, sin_ref, q_ref, k_ref, v_ref, g_ref, *, head_qk):
    xb = x_ref[...].astype(BF16)
    cos = cos_ref[...]
    sin = sin_ref[...]
    q = jnp.dot(xb, wq_ref[...], preferred_element_type=F32)
    q_ref[...] = _rotate(q, cos, sin, head_qk).astype(BF16)
    k = jnp.dot(xb, wk_ref[...], preferred_element_type=F32)
    k_ref[...] = (_rotate(k, cos, sin, head_qk) * (head_qk ** -0.5)).astype(BF16)
    v_ref[...] = jnp.dot(xb, wv_ref[...], preferred_element_type=F32).astype(BF16)
    g_ref[...] = jnp.dot(xb, wg_ref[...], preferred_element_type=F32).astype(BF16)


def _qkvg(x, w_bf, qk, vd, cos, sin):
    t, d = x.shape
    head_qk = qk // RET_HEADS
    tm = TM_QKVG
    assert vd == 2 * qk
    tokb = lambda n: pl.BlockSpec((tm, n), lambda i: (i, 0))
    wcol = lambda n, j: pl.BlockSpec((d, n), lambda i: (0, j))
    return pl.pallas_call(
        functools.partial(_qkvg_kernel, head_qk=head_qk),
        grid=(t // tm,),
        in_specs=[tokb(d), wcol(qk, 0), wcol(qk, 1), wcol(vd, 1), wcol(vd, 2), tokb(head_qk // 2), tokb(head_qk // 2)],
        out_specs=(tokb(qk), tokb(qk), tokb(vd), tokb(vd)),
        out_shape=(jax.ShapeDtypeStruct((t, qk), BF16), jax.ShapeDtypeStruct((t, qk), BF16),
                   jax.ShapeDtypeStruct((t, vd), BF16), jax.ShapeDtypeStruct((t, vd), BF16)),
        compiler_params=_cparams(("arbitrary",)),
        name="ret_qkvg_rope",
    )(x, w_bf, w_bf, w_bf, w_bf, cos, sin)


def _ret_core_kernel(q_ref, k_ref, v_ref, g_ref, mask_ref, xi_ref, zeta_ref, dec_ref, gng_ref, gnb_ref,
                     y_ref, state_ref):
    blk = RET_BLOCK
    n_blk = q_ref.shape[0] // blk

    @pl.when(pl.program_id(2) == 0)
    def _():
        state_ref[...] = jnp.zeros_like(state_ref)

    mask = mask_ref[...]
    xi = xi_ref[...]
    zeta = zeta_ref[...]
    dec = dec_ref[0:1, 0:1]
    for n in range(n_blk):
        rows = slice(n * blk, (n + 1) * blk)
        q = q_ref[rows, :]
        k = k_ref[rows, :]
        v = v_ref[rows, :]
        state = state_ref[...]
        s = lax.dot_general(q, k, (((1,), (1,)), ((), ())), preferred_element_type=F32)
        p = (s * mask).astype(BF16)
        o = jnp.dot(p, v, preferred_element_type=F32)
        qx = (q.astype(F32) * xi).astype(BF16)
        o = o + jnp.dot(qx, state.astype(BF16), preferred_element_type=F32)
        kz = (k.astype(F32) * zeta).astype(BF16)
        state_ref[...] = state * dec + lax.dot_general(kz, v, (((0,), (0,)), ((), ())),
                                                       preferred_element_type=F32)
        mu = jnp.mean(o, axis=-1, keepdims=True)
        oc = o - mu
        var = jnp.mean(oc * oc, axis=-1, keepdims=True)
        on = oc * lax.rsqrt(var + LN_EPS) * gng_ref[...] + gnb_ref[...]
        y_ref[rows, :] = (_silu(g_ref[rows, :].astype(F32)) * on).astype(BF16)


def _ret_tables():
    blk = RET_BLOCK
    log_gamma = jnp.log(1.0 - 2.0 ** (-5.0 - jnp.arange(RET_HEADS, dtype=F32)))
    idx = jnp.arange(blk, dtype=F32)
    dist = jnp.abs(idx[:, None] - idx[None, :])
    visible = (jnp.floor(idx[None, :] / RET_CHUNK) <= jnp.floor(idx[:, None] / RET_CHUNK))
    mask = jnp.where(visible[None], jnp.exp(log_gamma[:, None, None] * dist[None]), 0.0)
    xi = jnp.exp(log_gamma[:, None] * (idx[None, :] + 1.0))[..., None]
    zeta = jnp.exp(log_gamma[:, None] * (blk - 1.0 - idx[None, :]))[..., None]
    dec = jnp.broadcast_to(jnp.exp(log_gamma * blk)[:, None, None], (RET_HEADS, SUBLANES, LANES))
    return mask.astype(F32), xi.astype(F32), zeta.astype(F32), dec.astype(F32)


def _ret_core(q, k, v, g, gn_g, gn_b, batch, seq):
    t, qk = q.shape
    vd = v.shape[1]
    hq = qk // RET_HEADS
    hv = vd // RET_HEADS
    sb = RET_SUPER
    ns = seq // sb
    mask, xi, zeta, dec = _ret_tables()
    tokb = lambda n: pl.BlockSpec((sb, n), lambda b, h, s: (b * ns + s, h))
    headb = lambda r, c: pl.BlockSpec((None, r, c), lambda b, h, s: (h, 0, 0))
    return pl.pallas_call(
        _ret_core_kernel,
        grid=(batch, RET_HEADS, ns),
        in_specs=[tokb(hq), tokb(hq), tokb(hv), tokb(hv),
                  headb(RET_BLOCK, RET_BLOCK), headb(RET_BLOCK, 1), headb(RET_BLOCK, 1), headb(SUBLANES, LANES),
                  pl.BlockSpec((1, hv), lambda b, h, s: (0, h)), pl.BlockSpec((1, hv), lambda b, h, s: (0, h))],
        out_specs=tokb(hv),
        out_shape=jax.ShapeDtypeStruct((t, vd), BF16),
        scratch_shapes=[pltpu.VMEM((hq, hv), F32)],
        compiler_params=_cparams(("arbitrary", "arbitrary", "arbitrary")),
        name="ret_core",
    )(q, k, v, g, mask, xi, zeta, dec, gn_g.reshape(1, -1), gn_b.reshape(1, -1))


def _ret_tail_kernel(y_ref, x_ref, wo_ref, l1g_ref, l1b_ref, wrh_ref, wrl_ref, br_ref,
                     x1_ref, xs_ref, rg_ref, cw_ref):
    mix = jnp.dot(y_ref[...], wo_ref[...], preferred_element_type=F32)
    pre = DEEPNORM_ALPHA * x_ref[...] + mix
    _ln1_route_sort(pre, l1g_ref, l1b_ref, wrh_ref, wrl_ref, br_ref, x1_ref, xs_ref, rg_ref, cw_ref)


def _ret_tail(y, x, w_o_bf, l1g, l1b, wrh, wrl, br):
    t, d = x.shape
    vd = y.shape[1]
    ts = TS_STEP
    row = lambda v: v.reshape(1, -1)
    const = lambda shape: pl.BlockSpec(shape, lambda i: (0, 0))
    tok = lambda i: (i, 0)
    return pl.pallas_call(
        _ret_tail_kernel,
        grid=(t // ts,),
        in_specs=[pl.BlockSpec((ts, vd), tok), pl.BlockSpec((ts, d), tok), const((vd, d)),
                  const((1, d)), const((1, d)), const((d, LANES)), const((d, LANES)), const((1, LANES))],
        out_specs=_tail_out_specs(d, tok),
        out_shape=_tail_out_shapes(t, d),
        compiler_params=_cparams(("arbitrary",)),
        name="ret_tail_ln1_route",
    )(y, x, w_o_bf, row(l1g), row(l1b), wrh, wrl, br)


def _gather_groups(src_hbm, table_ref, first, n_groups, dst_ref, sem):
    def body(j, carry):
        pltpu.make_async_copy(_row_group(src_hbm, table_ref[first + j]),
                              _row_group(dst_ref, j * SUBLANES), sem).start()
        return carry

    lax.fori_loop(0, n_groups, body, 0, unroll=8)


def _wait_groups(src_hbm, dst_ref, sem):
    pltpu.make_async_copy(src_hbm.at[pl.ds(0, dst_ref.shape[0])], dst_ref, sem).wait()


def _ffn_kernel(gsrc_ref, tile_e_ref, nt_ref, xs_hbm, wg_ref, wu_ref, wd_ref, o_ref, xbuf_ref, sem):
    del tile_e_ref
    i = pl.program_id(0)
    nt = nt_ref[0]
    slot = i % 2

    @pl.when(i == 0)
    def _():
        _gather_groups(xs_hbm, gsrc_ref, 0, TILE_GROUPS, xbuf_ref.at[0], sem.at[0])

    @pl.when(i < nt)
    def _():
        _wait_groups(xs_hbm, xbuf_ref.at[slot], sem.at[slot])

        @pl.when(i + 1 < nt)
        def _():
            _gather_groups(xs_hbm, gsrc_ref, (i + 1) * TILE_GROUPS, TILE_GROUPS,
                           xbuf_ref.at[1 - slot], sem.at[1 - slot])

        x = xbuf_ref[slot].astype(BF16)
        a = jnp.dot(x, wg_ref[...].astype(BF16), preferred_element_type=F32)
        u = jnp.dot(x, wu_ref[...].astype(BF16), preferred_element_type=F32)
        h = (_silu(a) * u).astype(BF16)
        o_ref[...] = jnp.dot(h, wd_ref[...].astype(BF16), preferred_element_type=F32)

    @pl.when(i >= nt)
    def _():
        o_ref[...] = jnp.zeros_like(o_ref)


def _ffn(xs, gsrc, tile_e, n_tiles, w_gate, w_up, w_down, nt_max):
    d = xs.shape[1]
    f = w_gate.shape[-1]
    tm = TM_FFN
    grid_spec = pltpu.PrefetchScalarGridSpec(
        num_scalar_prefetch=3,
        grid=(nt_max,),
        in_specs=[pl.BlockSpec(memory_space=pl.ANY),
                  pl.BlockSpec((None, d, f), lambda i, gs, te, nt: (te[i], 0, 0)),
                  pl.BlockSpec((None, d, f), lambda i, gs, te, nt: (te[i], 0, 0)),
                  pl.BlockSpec((None, f, d), lambda i, gs, te, nt: (te[i], 0, 0))],
        out_specs=pl.BlockSpec((tm, d), lambda i, gs, te, nt: (i, 0)),
        scratch_shapes=[pltpu.VMEM((2, tm, d), F32), pltpu.SemaphoreType.DMA((2,))],
    )
    return pl.pallas_call(
        _ffn_kernel,
        grid_spec=grid_spec,
        out_shape=jax.ShapeDtypeStruct((nt_max * tm, d), F32),
        compiler_params=_cparams(("arbitrary",)),
        name="moe_expert_ffn",
    )(gsrc, tile_e, n_tiles, xs, w_gate, w_up, w_down)


def _combine_ln2_kernel(csrc_ref, x1_ref, rg_ref, l2g_ref, l2b_ref, ys_hbm, o_ref, ybuf_ref, sem):
    i = pl.program_id(0)
    n = pl.num_programs(0)
    slot = i % 2

    @pl.when(i == 0)
    def _():
        _gather_groups(ys_hbm, csrc_ref, 0, WIN_GROUPS, ybuf_ref.at[0], sem.at[0])

    _wait_groups(ys_hbm, ybuf_ref.at[slot], sem.at[slot])

    @pl.when(i + 1 < n)
    def _():
        _gather_groups(ys_hbm, csrc_ref, (i + 1) * WIN_GROUPS, WIN_GROUPS, ybuf_ref.at[1 - slot], sem.at[1 - slot])

    info = rg_ref[...]
    g0, g1, lp0, lp1 = info[:, 0:1], info[:, 1:2], info[:, 2:3], info[:, 3:4]
    ts = info.shape[0]
    srow = lax.broadcasted_iota(I32, (ts, WIN_ROWS), 1).astype(F32)
    unsort = (jnp.where(srow == lp0, g0, 0.0) + jnp.where(srow == lp1, g1, 0.0)).astype(BF16)
    ffn = jnp.dot(unsort, ybuf_ref[slot].astype(BF16), preferred_element_type=F32)
    o_ref[...] = _ln(DEEPNORM_ALPHA * x1_ref[...] + ffn, l2g_ref[...], l2b_ref[...])


def _combine_ln2(x1, rg, csrc, ys, l2g, l2b):
    t, d = x1.shape
    ts = TS_TAIL
    grid_spec = pltpu.PrefetchScalarGridSpec(
        num_scalar_prefetch=1,
        grid=(t // ts,),
        in_specs=[pl.BlockSpec((ts, d), lambda i, cs: (i, 0)),
                  pl.BlockSpec((ts, LANES), lambda i, cs: (i, 0)),
                  pl.BlockSpec((1, d), lambda i, cs: (0, 0)),
                  pl.BlockSpec((1, d), lambda i, cs: (0, 0)),
                  pl.BlockSpec(memory_space=pl.ANY)],
        out_specs=pl.BlockSpec((ts, d), lambda i, cs: (i, 0)),
        scratch_shapes=[pltpu.VMEM((2, WIN_ROWS, d), F32), pltpu.SemaphoreType.DMA((2,))],
    )
    return pl.pallas_call(
        _combine_ln2_kernel,
        grid_spec=grid_spec,
        out_shape=jax.ShapeDtypeStruct((t, d), F32),
        compiler_params=_cparams(("arbitrary",)),
        name="moe_combine_ln2",
    )(csrc, x1, rg, l2g.reshape(1, -1), l2b.reshape(1, -1), ys)


def _moe_tables(cw, nw, nt_max):
    tm = TM_FFN
    cnt = cw.reshape(nw, SUBLANES, LANES)[:, 0, ROUTE_LANE0:ROUTE_LANE0 + N_EXPERTS].astype(I32)
    run = ((cnt + SUBLANES - 1) // SUBLANES) * SUBLANES
    loc = jnp.cumsum(run, axis=1) - run
    e_rows = jnp.sum(run, axis=0)
    e_pad = ((e_rows + tm - 1) // tm) * tm
    e_end = jnp.cumsum(e_pad)
    glob = (e_end - e_pad)[None, :] + jnp.cumsum(run, axis=0) - run

    starts = glob.T.reshape(-1)
    lens = run.T.reshape(-1)
    srcs = (jnp.arange(nw, dtype=I32)[:, None] * WIN_ROWS + loc).T.reshape(-1)
    rows = jnp.arange(nt_max * TILE_GROUPS, dtype=I32) * SUBLANES
    idx = jnp.maximum(jnp.sum((starts[None, :] <= rows[:, None]).astype(I32), axis=1) - 1, 0)
    delta = rows - jnp.take(starts, idx)
    zero_group = WIN_ROWS - SUBLANES
    gsrc = jnp.where(delta < jnp.take(lens, idx), jnp.take(srcs, idx) + delta, zero_group).astype(I32)
    tile_start = jnp.arange(nt_max, dtype=I32) * tm
    tile_e = jnp.minimum(jnp.sum((e_end[None, :] <= tile_start[:, None]).astype(I32), axis=1), N_EXPERTS - 1)
    n_tiles = (e_end[-1:] // tm).astype(I32)

    lrow = jnp.arange(WIN_GROUPS, dtype=I32) * SUBLANES
    e_idx = jnp.maximum(jnp.sum((loc[:, None, :] <= lrow[None, :, None]).astype(I32), axis=2) - 1, 0)
    loc_e = jnp.take_along_axis(loc, e_idx, axis=1)
    ldelta = lrow[None, :] - loc_e
    valid = ldelta < jnp.take_along_axis(run, e_idx, axis=1)
    csrc = jnp.where(valid, jnp.take_along_axis(glob, e_idx, axis=1) + ldelta, 0).astype(I32).reshape(-1)
    return gsrc, tile_e.astype(I32), n_tiles, csrc


def _moe(x1, xs, rg, cw, w_gate, w_up, w_down, l2g, l2b):
    t = x1.shape[0]
    nw = t // TS_TAIL
    max_rows = TOP_K * t + nw * N_EXPERTS * (SUBLANES - 1) + N_EXPERTS * (TM_FFN - 1)
    nt_max = -(-max_rows // TM_FFN)
    gsrc, tile_e, n_tiles, csrc = _moe_tables(cw, nw, nt_max)
    ys = _ffn(xs, gsrc, tile_e, n_tiles, w_gate, w_up, w_down, nt_max)
    return _combine_ln2(x1, rg, csrc, ys, l2g, l2b)


def _router_weights(w_grp, b_grp, w_route, b_route):
    d = w_grp.shape[0]
    used = N_GROUPS + N_EXPERTS
    w = jnp.concatenate([w_grp, w_route, jnp.zeros((d, LANES - used), F32)], axis=1)
    b = jnp.concatenate([b_grp, b_route, jnp.zeros((LANES - used,), F32)]).reshape(1, LANES)
    wh = w.astype(BF16)
    wl = (w - wh.astype(F32)).astype(BF16)
    return wh, wl, b


def kernel(x, positions, conv_w_pw1, conv_b_pw1, conv_w_dw, conv_b_dw, conv_ln_g, conv_ln_b, conv_w_pw2, conv_b_pw2,
           ret_w_qkvg, ret_gn_g, ret_gn_b, ret_w_o, ln1_g, ln1_b, ln2_g, ln2_b, moe_w_grp, moe_b_grp, moe_w_route,
           moe_b_route, moe_w_gate, moe_w_up, moe_w_down):
    batch, seq, d = x.shape
    t = batch * seq
    qk = d
    vd = 2 * d
    xt = x.reshape(t, d)
    cos, sin = _rope_tables(positions, qk // RET_HEADS // 2)
    for i in range(DEPTH):
        j = i // N_MIXERS
        wrh, wrl, br = _router_weights(moe_w_grp[i], moe_b_grp[i], moe_w_route[i], moe_b_route[i])
        if i % N_MIXERS == 0:
            h = _pw1_glu(xt, conv_w_pw1[j].astype(BF16), conv_b_pw1[j])
            x1, xs, rg, cw = _conv_tail(h, xt, batch, seq, conv_w_dw[j], conv_b_dw[j], conv_ln_g[j], conv_ln_b[j],
                                        conv_w_pw2[j].astype(BF16), conv_b_pw2[j], ln1_g[i], ln1_b[i], wrh, wrl, br)
        else:
            q, k, v, g = _qkvg(xt, ret_w_qkvg[j].astype(BF16), qk, vd, cos, sin)
            y = _ret_core(q, k, v, g, ret_gn_g[j], ret_gn_b[j], batch, seq)
            x1, xs, rg, cw = _ret_tail(y, xt, ret_w_o[j].astype(BF16), ln1_g[i], ln1_b[i], wrh, wrl, br)
        xt = _moe(x1, xs, rg, cw, moe_w_gate[i], moe_w_up[i], moe_w_down[i], ln2_g[i], ln2_b[i])
    return xt.reshape(batch, seq, d)
```

```python
import functools

import jax
import jax.numpy as jnp
from jax import lax
from jax.experimental import pallas as pl
from jax.experimental.pallas import tpu as pltpu

F32 = jnp.float32
BF16 = jnp.bfloat16
I32 = jnp.int32

DEPTH = 4
N_MIXERS = 2
CONV_WIDTH = 31
RET_HEADS = 4
RET_CHUNK = 64
N_GROUPS = 4
EXPERTS_PER_GROUP = 8
N_EXPERTS = N_GROUPS * EXPERTS_PER_GROUP
TOP_K = 2
ROPE_BASE = 10000.0
DEEPNORM_ALPHA = (2.0 * DEPTH) ** 0.25
LN_EPS = 1e-5

LANES = 128
SUBLANES = 8
VMEM_LIMIT = 56 * 1024 * 1024

HALO = 32
CONV_ROWS = 64
TS_TAIL = 256
TAIL_WINDOWS = 2
TS_STEP = TAIL_WINDOWS * TS_TAIL
TM_PW1 = 512
TM_QKVG = 256
RET_BLOCK = 256
RET_SUPER = 1024
TM_FFN = 512
ROUTE_ROWS = LANES
ROUTE_ROW0 = N_GROUPS
WIN_ROWS = -(-(TOP_K * TS_TAIL + N_EXPERTS * (SUBLANES - 1) + SUBLANES) // LANES) * LANES
WIN_GROUPS = WIN_ROWS // SUBLANES
TILE_GROUPS = TM_FFN // SUBLANES

_NT = (((1,), (1,)), ((), ()))
_TN = (((0,), (0,)), ((), ()))


def _cparams(sem):
    return pltpu.CompilerParams(dimension_semantics=sem, vmem_limit_bytes=VMEM_LIMIT)


def _ln(x, g, b):
    mu = jnp.mean(x, axis=-1, keepdims=True)
    xc = x - mu
    var = jnp.mean(xc * xc, axis=-1, keepdims=True)
    return xc * lax.rsqrt(var + LN_EPS) * g + b


def _silu(x):
    return x * jax.nn.sigmoid(x)


def _row_group(ref, row):
    return ref.at[pl.ds(pl.multiple_of(row, SUBLANES), SUBLANES)]


def _ln1_route_sort(pre, l1g_ref, l1b_ref, wrh_ref, wrl_ref, br_ref, x1_ref, xs_ref, rg_ref, cw_ref):
    for w in range(pre.shape[0] // TS_TAIL):
        rows = slice(w * TS_TAIL, (w + 1) * TS_TAIL)
        _ln1_route_sort_window(pre[rows, :], l1g_ref, l1b_ref, wrh_ref, wrl_ref, br_ref, x1_ref.at[rows, :],
                               xs_ref.at[w * WIN_ROWS:(w + 1) * WIN_ROWS, :], rg_ref.at[:, rows],
                               cw_ref.at[w * ROUTE_ROWS:(w + 1) * ROUTE_ROWS, :])


def _ln1_route_sort_window(pre, l1g_ref, l1b_ref, wrh_ref, wrl_ref, br_ref, x1_ref, xs_ref, rg_ref, cw_ref):
    ts = pre.shape[0]
    x1 = _ln(pre, l1g_ref[...], l1b_ref[...])
    x1_ref[...] = x1

    xh = x1.astype(BF16)
    xl = (x1 - xh.astype(F32)).astype(BF16)
    wrh = wrh_ref[...]
    logits = (lax.dot_general(wrh, xh, _NT, preferred_element_type=F32)
              + lax.dot_general(wrh, xl, _NT, preferred_element_type=F32)
              + lax.dot_general(wrl_ref[...], xh, _NT, preferred_element_type=F32)) + br_ref[...]

    row = lax.broadcasted_iota(I32, logits.shape, 0).astype(F32)
    neg = jnp.float32(-jnp.inf)
    no_row = jnp.float32(ROUTE_ROWS)
    is_grp = row < N_GROUPS
    gl = jnp.where(is_grp, logits, neg)
    gm = jnp.max(gl, axis=0, keepdims=True)
    gidx = jnp.min(jnp.where(gl == gm, row, no_row), axis=0, keepdims=True)
    denom = jnp.sum(jnp.where(is_grp, jnp.exp(gl - gm), 0.0), axis=0, keepdims=True)
    p_g = 1.0 / denom

    lo = ROUTE_ROW0 + EXPERTS_PER_GROUP * gidx
    sel = (row >= lo) & (row < lo + EXPERTS_PER_GROUP)
    sl = jnp.where(sel, logits, neg)
    m1 = jnp.max(sl, axis=0, keepdims=True)
    i1 = jnp.min(jnp.where(sl == m1, row, no_row), axis=0, keepdims=True)
    sl2 = jnp.where(row == i1, neg, sl)
    m2 = jnp.max(sl2, axis=0, keepdims=True)
    i2 = jnp.min(jnp.where(sl2 == m2, row, no_row), axis=0, keepdims=True)
    e21 = jnp.exp(m2 - m1)
    g0 = p_g / (1.0 + e21)
    g1 = p_g * e21 / (1.0 + e21)

    oh0 = row == i1
    oh1 = row == i2
    s_f = jnp.where(oh0 | oh1, 1.0, 0.0)
    rr = lax.broadcasted_iota(I32, (ts, ts), 0)
    cc = lax.broadcasted_iota(I32, (ts, ts), 1)
    earlier = jnp.where(rr < cc, 1.0, 0.0).astype(BF16)
    rank = jnp.dot(s_f.astype(BF16), earlier, preferred_element_type=F32)
    cnt = jnp.sum(s_f, axis=1, keepdims=True)
    run = jnp.ceil(cnt * (1.0 / SUBLANES)) * SUBLANES
    er = lax.broadcasted_iota(I32, (ROUTE_ROWS, ROUTE_ROWS), 0)
    ec = lax.broadcasted_iota(I32, (ROUTE_ROWS, ROUTE_ROWS), 1)
    lower = jnp.where(ec < er, 1.0, 0.0).astype(BF16)
    run_start = jnp.dot(lower, jnp.broadcast_to(run, (ROUTE_ROWS, LANES)).astype(BF16),
                        preferred_element_type=F32)[:, 0:1]
    where_to = rank + run_start
    lp0 = jnp.sum(jnp.where(oh0, where_to, 0.0), axis=0, keepdims=True)
    lp1 = jnp.sum(jnp.where(oh1, where_to, 0.0), axis=0, keepdims=True)

    srow = lax.broadcasted_iota(I32, (WIN_ROWS, ts), 0).astype(F32)
    place = jnp.where((srow == lp0) | (srow == lp1), 1.0, 0.0).astype(BF16)
    xs_ref[...] = jnp.dot(place, xh, preferred_element_type=F32)

    r8 = lax.broadcasted_iota(I32, (SUBLANES, ts), 0)
    rg_ref[...] = jnp.where(r8 == 0, g0, jnp.where(r8 == 1, g1, jnp.where(r8 == 2, lp0,
                            jnp.where(r8 == 3, lp1, 0.0))))
    cw_ref[...] = jnp.broadcast_to(cnt, cw_ref.shape)


def _tail_out_shapes(t, d):
    nw = t // TS_TAIL
    return (jax.ShapeDtypeStruct((t, d), F32),
            jax.ShapeDtypeStruct((nw * WIN_ROWS, d), F32),
            jax.ShapeDtypeStruct((SUBLANES, t), F32),
            jax.ShapeDtypeStruct((nw * ROUTE_ROWS, SUBLANES), F32))


def _tail_out_specs(d, tok, tok_t):
    return (pl.BlockSpec((TS_STEP, d), tok),
            pl.BlockSpec((TAIL_WINDOWS * WIN_ROWS, d), tok),
            pl.BlockSpec((SUBLANES, TS_STEP), tok_t),
            pl.BlockSpec((TAIL_WINDOWS * ROUTE_ROWS, SUBLANES), tok))


def _pw1_glu_kernel(x_ref, w_ref, b_ref, o_ref):
    d = o_ref.shape[-1]
    h = jnp.dot(x_ref[...].astype(BF16), w_ref[...], preferred_element_type=F32) + b_ref[...]
    o_ref[...] = h[:, :d] * jax.nn.sigmoid(h[:, d:])


def _pw1_glu(x, w_bf, b):
    t, d = x.shape
    tm = TM_PW1
    return pl.pallas_call(
        _pw1_glu_kernel,
        grid=(t // tm,),
        in_specs=[pl.BlockSpec((tm, d), lambda i: (i, 0)),
                  pl.BlockSpec((d, 2 * d), lambda i: (0, 0)),
                  pl.BlockSpec((1, 2 * d), lambda i: (0, 0))],
        out_specs=pl.BlockSpec((tm, d), lambda i: (i, 0)),
        out_shape=jax.ShapeDtypeStruct((t, d), F32),
        compiler_params=_cparams(("arbitrary",)),
        name="conv_pw1_glu",
    )(x, w_bf, b.reshape(1, -1))


def _conv_tail_kernel(hcur_ref, hprev_ref, x_ref, wdw_ref, bdw_ref, lng_ref, lnb_ref, wpw2_ref, bpw2_ref,
                      l1g_ref, l1b_ref, wrh_ref, wrl_ref, br_ref,
                      x1_ref, xs_ref, rg_ref, cw_ref,
                      hext_ref, hsh_ref, conv_ref):
    ts, d = hcur_ref.shape
    j = pl.program_id(1)

    hext_ref[HALO:, :] = hcur_ref[...]

    @pl.when(j == 0)
    def _():
        hext_ref[0:HALO, :] = jnp.zeros((HALO, d), F32)

    @pl.when(j > 0)
    def _():
        hext_ref[0:HALO, :] = hprev_ref[...]

    n_sh = ts + HALO - SUBLANES
    offs = [HALO - (CONV_WIDTH - 1) + k for k in range(CONV_WIDTH)]

    for c in range(d // LANES):
        lanes = slice(c * LANES, (c + 1) * LANES)
        for b in range(1, SUBLANES):
            hsh_ref[b - 1] = hext_ref[b:b + n_sh, lanes]
        w_rows = [jnp.broadcast_to(wdw_ref[k:k + 1, lanes], (SUBLANES, LANES)) for k in range(CONV_WIDTH)]
        bias = jnp.broadcast_to(bdw_ref[:, lanes], (SUBLANES, LANES))

        def conv_rows(r, carry, lanes=lanes, w_rows=w_rows, bias=bias):
            r0 = pl.multiple_of(r * CONV_ROWS, CONV_ROWS)
            for grp in range(CONV_ROWS // SUBLANES):
                acc = bias
                for k, off in enumerate(offs):
                    start = r0 + (grp + off // SUBLANES) * SUBLANES
                    if off % SUBLANES == 0:
                        tap = hext_ref[pl.ds(start, SUBLANES), lanes]
                    else:
                        tap = hsh_ref[off % SUBLANES - 1, pl.ds(start, SUBLANES), :]
                    acc = acc + w_rows[k] * tap
                conv_ref[pl.ds(r0 + grp * SUBLANES, SUBLANES), lanes] = acc
            return carry

        lax.fori_loop(0, ts // CONV_ROWS, conv_rows, 0)

    hn = _silu(_ln(conv_ref[...], lng_ref[...], lnb_ref[...]))
    mix = jnp.dot(hn.astype(BF16), wpw2_ref[...], preferred_element_type=F32) + bpw2_ref[...]
    pre = DEEPNORM_ALPHA * x_ref[...] + mix
    _ln1_route_sort(pre, l1g_ref, l1b_ref, wrh_ref, wrl_ref, br_ref, x1_ref, xs_ref, rg_ref, cw_ref)


def _conv_tail(h, x, batch, seq, w_dw, b_dw, ln_g, ln_b, w_pw2_bf, b_pw2, l1g, l1b, wrh, wrl, br):
    t, d = x.shape
    ts = TS_STEP
    nj = seq // ts
    halo_per_tile = ts // HALO
    h3 = h.reshape(batch, seq, d)
    x3 = x.reshape(batch, seq, d)
    row = lambda v: v.reshape(1, -1)
    const2 = lambda shape: pl.BlockSpec(shape, lambda b, j: (0, 0))
    tok = lambda b, j: (b * nj + j, 0)
    tok_t = lambda b, j: (0, b * nj + j)
    return pl.pallas_call(
        _conv_tail_kernel,
        grid=(batch, nj),
        in_specs=[pl.BlockSpec((None, ts, d), lambda b, j: (b, j, 0)),
                  pl.BlockSpec((None, HALO, d), lambda b, j: (b, jnp.maximum(j * halo_per_tile - 1, 0), 0)),
                  pl.BlockSpec((None, ts, d), lambda b, j: (b, j, 0)),
                  const2((CONV_WIDTH, d)), const2((1, d)), const2((1, d)), const2((1, d)),
                  const2((d, d)), const2((1, d)), const2((1, d)), const2((1, d)),
                  const2((ROUTE_ROWS, d)), const2((ROUTE_ROWS, d)), const2((ROUTE_ROWS, 1))],
        out_specs=_tail_out_specs(d, tok, tok_t),
        out_shape=_tail_out_shapes(t, d),
        scratch_shapes=[pltpu.VMEM((ts + HALO, d), F32), pltpu.VMEM((SUBLANES - 1, ts + HALO - SUBLANES, LANES), F32),
                        pltpu.VMEM((ts, d), F32)],
        compiler_params=_cparams(("arbitrary", "arbitrary")),
        name="conv_tail_ln1_route",
    )(h3, h3, x3, w_dw, row(b_dw), row(ln_g), row(ln_b), w_pw2_bf, row(b_pw2), row(l1g), row(l1b), wrh, wrl, br)


def _rope_table_kernel(pos_ref, invf_ref, cos_ref, sin_ref):
    ang = pos_ref[...].astype(F32) * invf_ref[...]
    cos_ref[...] = jnp.cos(ang)
    sin_ref[...] = jnp.sin(ang)


def _rope_tables(positions, half):
    t = positions.size
    tm = 1024
    inv_freq = ROPE_BASE ** (-jnp.arange(half, dtype=F32) / half)
    return pl.pallas_call(
        _rope_table_kernel,
        grid=(t // tm,),
        in_specs=[pl.BlockSpec((tm, 1), lambda i: (i, 0)), pl.BlockSpec((1, half), lambda i: (0, 0))],
        out_specs=(pl.BlockSpec((tm, half), lambda i: (i, 0)), pl.BlockSpec((tm, half), lambda i: (i, 0))),
        out_shape=(jax.ShapeDtypeStruct((t, half), F32), jax.ShapeDtypeStruct((t, half), F32)),
        compiler_params=_cparams(("arbitrary",)),
        name="rope_tables",
    )(positions.reshape(t, 1), inv_freq.reshape(1, half))


def _rotate(t, cos, sin, head_dim):
    half = head_dim // 2
    parts = []
    for h in range(t.shape[1] // head_dim):
        t1 = t[:, h * head_dim:h * head_dim + half]
        t2 = t[:, h * head_dim + half:(h + 1) * head_dim]
        parts.append(t1 * cos - t2 * sin)
        parts.append(t1 * sin + t2 * cos)
    return jnp.concatenate(parts, axis=-1)


def _qkvg_kernel(x_ref, wq_ref, wk_ref, wv_ref, wg_ref, cos_ref, sin_ref, q_ref, k_ref, v_ref, g_ref, *, head_qk):
    xb = x_ref[...].astype(BF16)
    cos = cos_ref[...]
    sin = sin_ref[...]
    q = jnp.dot(xb, wq_ref[...], preferred_element_type=F32)
    q_ref[...] = _rotate(q, cos, sin, head_qk).astype(BF16)
    k = jnp.dot(xb, wk_ref[...], preferred_element_type=F32)
    k_ref[...] = (_rotate(k, cos, sin, head_qk) * (head_qk ** -0.5)).astype(BF16)
    v_ref[...] = jnp.dot(xb, wv_ref[...], preferred_element_type=F32).astype(BF16)
    g_ref[...] = jnp.dot(xb, wg_ref[...], preferred_element_type=F32).astype(BF16)


def _qkvg(x, w_bf, qk, vd, cos, sin):
    t, d = x.shape
    head_qk = qk // RET_HEADS
    tm = TM_QKVG
    assert vd == 2 * qk
    tokb = lambda n: pl.BlockSpec((tm, n), lambda i: (i, 0))
    wcol = lambda n, j: pl.BlockSpec((d, n), lambda i: (0, j))
    return pl.pallas_call(
        functools.partial(_qkvg_kernel, head_qk=head_qk),
        grid=(t // tm,),
        in_specs=[tokb(d), wcol(qk, 0), wcol(qk, 1), wcol(vd, 1), wcol(vd, 2), tokb(head_qk // 2), tokb(head_qk // 2)],
        out_specs=(tokb(qk), tokb(qk), tokb(vd), tokb(vd)),
        out_shape=(jax.ShapeDtypeStruct((t, qk), BF16), jax.ShapeDtypeStruct((t, qk), BF16),
                   jax.ShapeDtypeStruct((t, vd), BF16), jax.ShapeDtypeStruct((t, vd), BF16)),
        compiler_params=_cparams(("arbitrary",)),
        name="ret_qkvg_rope",
    )(x, w_bf, w_bf, w_bf, w_bf, cos, sin)


def _ret_core_kernel(q_ref, k_ref, v_ref, g_ref, mask_ref, xi_ref, zeta_ref, dec_ref, gng_ref, gnb_ref,
                     y_ref, state_ref):
    blk = RET_BLOCK
    n_blk = q_ref.shape[0] // blk

    @pl.when(pl.program_id(2) == 0)
    def _():
        state_ref[...] = jnp.zeros_like(state_ref)

    mask = mask_ref[...]
    xi = xi_ref[...]
    zeta = zeta_ref[...]
    dec = dec_ref[0:1, 0:1]
    for n in range(n_blk):
        rows = slice(n * blk, (n + 1) * blk)
        q = q_ref[rows, :]
        k = k_ref[rows, :]
        v = v_ref[rows, :]
        state = state_ref[...]
        s = lax.dot_general(q, k, _NT, preferred_element_type=F32)
        p = (s * mask).astype(BF16)
        o = jnp.dot(p, v, preferred_element_type=F32)
        qx = (q.astype(F32) * xi).astype(BF16)
        o = o + jnp.dot(qx, state.astype(BF16), preferred_element_type=F32)
        kz = (k.astype(F32) * zeta).astype(BF16)
        state_ref[...] = state * dec + lax.dot_general(kz, v, _TN, preferred_element_type=F32)
        mu = jnp.mean(o, axis=-1, keepdims=True)
        oc = o - mu
        var = jnp.mean(oc * oc, axis=-1, keepdims=True)
        on = oc * lax.rsqrt(var + LN_EPS) * gng_ref[...] + gnb_ref[...]
        y_ref[rows, :] = (_silu(g_ref[rows, :].astype(F32)) * on).astype(BF16)


def _ret_tables():
    blk = RET_BLOCK
    log_gamma = jnp.log(1.0 - 2.0 ** (-5.0 - jnp.arange(RET_HEADS, dtype=F32)))
    idx = jnp.arange(blk, dtype=F32)
    dist = jnp.abs(idx[:, None] - idx[None, :])
    visible = (jnp.floor(idx[None, :] / RET_CHUNK) <= jnp.floor(idx[:, None] / RET_CHUNK))
    mask = jnp.where(visible[None], jnp.exp(log_gamma[:, None, None] * dist[None]), 0.0)
    xi = jnp.exp(log_gamma[:, None] * (idx[None, :] + 1.0))[..., None]
    zeta = jnp.exp(log_gamma[:, None] * (blk - 1.0 - idx[None, :]))[..., None]
    dec = jnp.broadcast_to(jnp.exp(log_gamma * blk)[:, None, None], (RET_HEADS, SUBLANES, LANES))
    return mask.astype(F32), xi.astype(F32), zeta.astype(F32), dec.astype(F32)


def _ret_core(q, k, v, g, gn_g, gn_b, batch, seq):
    t, qk = q.shape
    vd = v.shape[1]
    hq = qk // RET_HEADS
    hv = vd // RET_HEADS
    sb = RET_SUPER
    ns = seq // sb
    mask, xi, zeta, dec = _ret_tables()
    tokb = lambda n: pl.BlockSpec((sb, n), lambda b, h, s: (b * ns + s, h))
    headb = lambda r, c: pl.BlockSpec((None, r, c), lambda b, h, s: (h, 0, 0))
    return pl.pallas_call(
        _ret_core_kernel,
        grid=(batch, RET_HEADS, ns),
        in_specs=[tokb(hq), tokb(hq), tokb(hv), tokb(hv),
                  headb(RET_BLOCK, RET_BLOCK), headb(RET_BLOCK, 1), headb(RET_BLOCK, 1), headb(SUBLANES, LANES),
                  pl.BlockSpec((1, hv), lambda b, h, s: (0, h)), pl.BlockSpec((1, hv), lambda b, h, s: (0, h))],
        out_specs=tokb(hv),
        out_shape=jax.ShapeDtypeStruct((t, vd), BF16),
        scratch_shapes=[pltpu.VMEM((hq, hv), F32)],
        compiler_params=_cparams(("arbitrary", "arbitrary", "arbitrary")),
        name="ret_core",
    )(q, k, v, g, mask, xi, zeta, dec, gn_g.reshape(1, -1), gn_b.reshape(1, -1))


def _ret_tail_kernel(y_ref, x_ref, wo_ref, l1g_ref, l1b_ref, wrh_ref, wrl_ref, br_ref,
                     x1_ref, xs_ref, rg_ref, cw_ref):
    mix = jnp.dot(y_ref[...], wo_ref[...], preferred_element_type=F32)
    pre = DEEPNORM_ALPHA * x_ref[...] + mix
    _ln1_route_sort(pre, l1g_ref, l1b_ref, wrh_ref, wrl_ref, br_ref, x1_ref, xs_ref, rg_ref, cw_ref)


def _ret_tail(y, x, w_o_bf, l1g, l1b, wrh, wrl, br):
    t, d = x.shape
    vd = y.shape[1]
    ts = TS_STEP
    row = lambda v: v.reshape(1, -1)
    const = lambda shape: pl.BlockSpec(shape, lambda i: (0, 0))
    tok = lambda i: (i, 0)
    tok_t = lambda i: (0, i)
    return pl.pallas_call(
        _ret_tail_kernel,
        grid=(t // ts,),
        in_specs=[pl.BlockSpec((ts, vd), tok), pl.BlockSpec((ts, d), tok), const((vd, d)),
                  const((1, d)), const((1, d)), const((ROUTE_ROWS, d)), const((ROUTE_ROWS, d)), const((ROUTE_ROWS, 1))],
        out_specs=_tail_out_specs(d, tok, tok_t),
        out_shape=_tail_out_shapes(t, d),
        compiler_params=_cparams(("arbitrary",)),
        name="ret_tail_ln1_route",
    )(y, x, w_o_bf, row(l1g), row(l1b), wrh, wrl, br)


def _gather_groups(src_hbm, table_ref, first, n_groups, dst_ref, sem):
    def body(j, carry):
        pltpu.make_async_copy(_row_group(src_hbm, table_ref[first + j]),
                              _row_group(dst_ref, j * SUBLANES), sem).start()
        return carry

    lax.fori_loop(0, n_groups, body, 0, unroll=8)


def _wait_groups(src_hbm, dst_ref, sem):
    pltpu.make_async_copy(src_hbm.at[pl.ds(0, dst_ref.shape[0])], dst_ref, sem).wait()


def _ffn_kernel(gsrc_ref, tile_e_ref, nt_ref, xs_hbm, wg_ref, wu_ref, wd_ref, o_ref,
                xbuf_ref, wgb_ref, wub_ref, wdb_ref, sem):
    i = pl.program_id(0)
    nt = nt_ref[0]
    slot = i % 2

    @pl.when(i == 0)
    def _():
        _gather_groups(xs_hbm, gsrc_ref, 0, TILE_GROUPS, xbuf_ref.at[0], sem.at[0])

    @pl.when(i < nt)
    def _():
        @pl.when((i == 0) | (tile_e_ref[i] != tile_e_ref[jnp.maximum(i - 1, 0)]))
        def _():
            wgb_ref[...] = wg_ref[...].astype(BF16)
            wub_ref[...] = wu_ref[...].astype(BF16)
            wdb_ref[...] = wd_ref[...].astype(BF16)

        _wait_groups(xs_hbm, xbuf_ref.at[slot], sem.at[slot])

        @pl.when(i + 1 < nt)
        def _():
            _gather_groups(xs_hbm, gsrc_ref, (i + 1) * TILE_GROUPS, TILE_GROUPS,
                           xbuf_ref.at[1 - slot], sem.at[1 - slot])

        x = xbuf_ref[slot].astype(BF16)
        a = jnp.dot(x, wgb_ref[...], preferred_element_type=F32)
        u = jnp.dot(x, wub_ref[...], preferred_element_type=F32)
        h = (_silu(a) * u).astype(BF16)
        o_ref[...] = jnp.dot(h, wdb_ref[...], preferred_element_type=F32)

    @pl.when(i >= nt)
    def _():
        o_ref[...] = jnp.zeros_like(o_ref)


def _ffn(xs, gsrc, tile_e, n_tiles, w_gate, w_up, w_down, layer, nt_max):
    d = xs.shape[1]
    f = w_gate.shape[-1]
    tm = TM_FFN
    wspec = lambda r, c: pl.BlockSpec((None, None, r, c), lambda i, gs, te, nt: (layer, te[i], 0, 0))
    grid_spec = pltpu.PrefetchScalarGridSpec(
        num_scalar_prefetch=3,
        grid=(nt_max,),
        in_specs=[pl.BlockSpec(memory_space=pl.ANY), wspec(d, f), wspec(d, f), wspec(f, d)],
        out_specs=pl.BlockSpec((tm, d), lambda i, gs, te, nt: (i, 0)),
        scratch_shapes=[pltpu.VMEM((2, tm, d), F32), pltpu.VMEM((d, f), BF16), pltpu.VMEM((d, f), BF16),
                        pltpu.VMEM((f, d), BF16), pltpu.SemaphoreType.DMA((2,))],
    )
    return pl.pallas_call(
        _ffn_kernel,
        grid_spec=grid_spec,
        out_shape=jax.ShapeDtypeStruct((nt_max * tm, d), F32),
        compiler_params=_cparams(("arbitrary",)),
        name="moe_expert_ffn",
    )(gsrc, tile_e, n_tiles, xs, w_gate, w_up, w_down)


def _combine_ln2_kernel(csrc_ref, x1_ref, rg_ref, l2g_ref, l2b_ref, ys_hbm, o_ref, ybuf_ref, sem):
    i = pl.program_id(0)
    n = pl.num_programs(0)
    slot = i % 2

    @pl.when(i == 0)
    def _():
        _gather_groups(ys_hbm, csrc_ref, 0, WIN_GROUPS, ybuf_ref.at[0], sem.at[0])

    _wait_groups(ys_hbm, ybuf_ref.at[slot], sem.at[slot])

    @pl.when(i + 1 < n)
    def _():
        _gather_groups(ys_hbm, csrc_ref, (i + 1) * WIN_GROUPS, WIN_GROUPS, ybuf_ref.at[1 - slot], sem.at[1 - slot])

    info = rg_ref[...]
    g0, g1, lp0, lp1 = info[0:1, :], info[1:2, :], info[2:3, :], info[3:4, :]
    ts = info.shape[1]
    srow = lax.broadcasted_iota(I32, (WIN_ROWS, ts), 0).astype(F32)
    unsort = (jnp.where(srow == lp0, g0, 0.0) + jnp.where(srow == lp1, g1, 0.0)).astype(BF16)
    ffn = lax.dot_general(unsort, ybuf_ref[slot].astype(BF16), _TN, preferred_element_type=F32)
    o_ref[...] = _ln(DEEPNORM_ALPHA * x1_ref[...] + ffn, l2g_ref[...], l2b_ref[...])


def _combine_ln2(x1, rg, csrc, ys, l2g, l2b):
    t, d = x1.shape
    ts = TS_TAIL
    grid_spec = pltpu.PrefetchScalarGridSpec(
        num_scalar_prefetch=1,
        grid=(t // ts,),
        in_specs=[pl.BlockSpec((ts, d), lambda i, cs: (i, 0)),
                  pl.BlockSpec((SUBLANES, ts), lambda i, cs: (0, i)),
                  pl.BlockSpec((1, d), lambda i, cs: (0, 0)),
                  pl.BlockSpec((1, d), lambda i, cs: (0, 0)),
                  pl.BlockSpec(memory_space=pl.ANY)],
        out_specs=pl.BlockSpec((ts, d), lambda i, cs: (i, 0)),
        scratch_shapes=[pltpu.VMEM((2, WIN_ROWS, d), F32), pltpu.SemaphoreType.DMA((2,))],
    )
    return pl.pallas_call(
        _combine_ln2_kernel,
        grid_spec=grid_spec,
        out_shape=jax.ShapeDtypeStruct((t, d), F32),
        compiler_params=_cparams(("arbitrary",)),
        name="moe_combine_ln2",
    )(csrc, x1, rg, l2g.reshape(1, -1), l2b.reshape(1, -1), ys)


def _pick_last(below, values, axis):
    first = lax.index_in_dim(values, 0, axis, keepdims=False)
    steps = lax.slice_in_dim(values, 1, None, axis=axis) - lax.slice_in_dim(values, 0, -1, axis=axis)
    return first + jnp.sum(jnp.where(below, steps, 0), axis=axis)


def _moe_tables(cw, nw, nt_max):
    tm = TM_FFN
    cnt = cw.reshape(nw, ROUTE_ROWS, SUBLANES)[:, ROUTE_ROW0:ROUTE_ROW0 + N_EXPERTS, 0].astype(I32)
    run = ((cnt + SUBLANES - 1) // SUBLANES) * SUBLANES
    loc = jnp.cumsum(run, axis=1) - run
    e_rows = jnp.sum(run, axis=0)
    e_pad = ((e_rows + tm - 1) // tm) * tm
    e_end = jnp.cumsum(e_pad)
    glob = (e_end - e_pad)[None, :] + jnp.cumsum(run, axis=0) - run

    starts = glob.T.reshape(-1)
    ends = starts + run.T.reshape(-1)
    shift = (jnp.arange(nw, dtype=I32)[:, None] * WIN_ROWS + loc).T.reshape(-1) - starts
    rows = jnp.arange(nt_max * TILE_GROUPS, dtype=I32) * SUBLANES
    below = starts[None, 1:] <= rows[:, None]
    zero_group = WIN_ROWS - SUBLANES
    gsrc = jnp.where(rows < _pick_last(below, ends[None, :], 1), rows + _pick_last(below, shift[None, :], 1),
                     zero_group).astype(I32)
    tile_start = jnp.arange(nt_max, dtype=I32) * tm
    tile_e = jnp.minimum(jnp.sum((e_end[None, :] <= tile_start[:, None]).astype(I32), axis=1), N_EXPERTS - 1)
    n_tiles = (e_end[-1:] // tm).astype(I32)

    lrow = jnp.arange(WIN_GROUPS, dtype=I32) * SUBLANES
    below_w = loc[:, None, 1:] <= lrow[None, :, None]
    run_end = _pick_last(below_w, (loc + run)[:, None, :], 2)
    to_glob = _pick_last(below_w, (glob - loc)[:, None, :], 2)
    csrc = jnp.where(lrow[None, :] < run_end, lrow[None, :] + to_glob, 0).astype(I32).reshape(-1)
    return gsrc, tile_e.astype(I32), n_tiles, csrc


def _moe(x1, xs, rg, cw, w_gate, w_up, w_down, layer, l2g, l2b):
    t = x1.shape[0]
    nw = t // TS_TAIL
    max_rows = TOP_K * t + nw * N_EXPERTS * (SUBLANES - 1) + N_EXPERTS * (TM_FFN - 1)
    nt_max = -(-max_rows // TM_FFN)
    gsrc, tile_e, n_tiles, csrc = _moe_tables(cw, nw, nt_max)
    ys = _ffn(xs, gsrc, tile_e, n_tiles, w_gate, w_up, w_down, layer, nt_max)
    return _combine_ln2(x1, rg, csrc, ys, l2g, l2b)


def _router_weights(w_grp, b_grp, w_route, b_route):
    d = w_grp.shape[0]
    used = N_GROUPS + N_EXPERTS
    w = jnp.concatenate([w_grp, w_route, jnp.zeros((d, ROUTE_ROWS - used), F32)], axis=1).T
    b = jnp.concatenate([b_grp, b_route, jnp.zeros((ROUTE_ROWS - used,), F32)]).reshape(ROUTE_ROWS, 1)
    wh = w.astype(BF16)
    wl = (w - wh.astype(F32)).astype(BF16)
    return wh, wl, b


def kernel(x, positions, conv_w_pw1, conv_b_pw1, conv_w_dw, conv_b_dw, conv_ln_g, conv_ln_b, conv_w_pw2, conv_b_pw2,
           ret_w_qkvg, ret_gn_g, ret_gn_b, ret_w_o, ln1_g, ln1_b, ln2_g, ln2_b, moe_w_grp, moe_b_grp, moe_w_route,
           moe_b_route, moe_w_gate, moe_w_up, moe_w_down):
    batch, seq, d = x.shape
    t = batch * seq
    qk = d
    vd = 2 * d
    xt = x.reshape(t, d)
    cos, sin = _rope_tables(positions, qk // RET_HEADS // 2)
    for i in range(DEPTH):
        j = i // N_MIXERS
        wrh, wrl, br = _router_weights(moe_w_grp[i], moe_b_grp[i], moe_w_route[i], moe_b_route[i])
        if i % N_MIXERS == 0:
            h = _pw1_glu(xt, conv_w_pw1[j].astype(BF16), conv_b_pw1[j])
            x1, xs, rg, cw = _conv_tail(h, xt, batch, seq, conv_w_dw[j], conv_b_dw[j], conv_ln_g[j], conv_ln_b[j],
                                        conv_w_pw2[j].astype(BF16), conv_b_pw2[j], ln1_g[i], ln1_b[i], wrh, wrl, br)
        else:
            q, k, v, g = _qkvg(xt, ret_w_qkvg[j].astype(BF16), qk, vd, cos, sin)
            y = _ret_core(q, k, v, g, ret_gn_g[j], ret_gn_b[j], batch, seq)
            x1, xs, rg, cw = _ret_tail(y, xt, ret_w_o[j].astype(BF16), ln1_g[i], ln1_b[i], wrh, wrl, br)
        xt = _moe(x1, xs, rg, cw, moe_w_gate, moe_w_up, moe_w_down, i, ln2_g[i], ln2_b[i])
    return xt.reshape(batch, seq, d)
```

```python
import functools

import jax
import jax.numpy as jnp
from jax import lax
from jax.experimental import pallas as pl
from jax.experimental.pallas import tpu as pltpu

F32 = jnp.float32
BF16 = jnp.bfloat16
I32 = jnp.int32

DEPTH = 4
N_MIXERS = 2
CONV_WIDTH = 31
RET_HEADS = 4
RET_CHUNK = 64
N_GROUPS = 4
EXPERTS_PER_GROUP = 8
N_EXPERTS = N_GROUPS * EXPERTS_PER_GROUP
TOP_K = 2
ROPE_BASE = 10000.0
DEEPNORM_ALPHA = (2.0 * DEPTH) ** 0.25
LN_EPS = 1e-5

LANES = 128
SUBLANES = 8
VMEM_LIMIT = 56 * 1024 * 1024

HALO = 32
CONV_ROWS = 64
TS_TAIL = 256
TAIL_WINDOWS = 2
TS_STEP = TAIL_WINDOWS * TS_TAIL
TM_PW1 = 512
TM_QKVG = 256
RET_BLOCK = 256
RET_SUPER = 1024
TM_FFN = 512
ROUTE_ROWS = LANES
ROUTE_ROW0 = N_GROUPS
WIN_ROWS = -(-(TOP_K * TS_TAIL + N_EXPERTS * (SUBLANES - 1) + SUBLANES) // LANES) * LANES
WIN_GROUPS = WIN_ROWS // SUBLANES
TILE_GROUPS = TM_FFN // SUBLANES

_NT = (((1,), (1,)), ((), ()))
_TN = (((0,), (0,)), ((), ()))


def _cparams(sem):
    return pltpu.CompilerParams(dimension_semantics=sem, vmem_limit_bytes=VMEM_LIMIT)


def _ln(x, g, b):
    mu = jnp.mean(x, axis=-1, keepdims=True)
    xc = x - mu
    var = jnp.mean(xc * xc, axis=-1, keepdims=True)
    return xc * lax.rsqrt(var + LN_EPS) * g + b


def _silu(x):
    return x * jax.nn.sigmoid(x)


def _row_group(ref, row):
    return ref.at[pl.ds(pl.multiple_of(row, SUBLANES), SUBLANES)]


def _ln1_route_sort(pre, l1g_ref, l1b_ref, wrh_ref, wrl_ref, br_ref, x1_ref, xs_ref, rg_ref, cw_ref):
    for w in range(pre.shape[0] // TS_TAIL):
        rows = slice(w * TS_TAIL, (w + 1) * TS_TAIL)
        _ln1_route_sort_window(pre[rows, :], l1g_ref, l1b_ref, wrh_ref, wrl_ref, br_ref, x1_ref.at[rows, :],
                               xs_ref.at[w * WIN_ROWS:(w + 1) * WIN_ROWS, :], rg_ref.at[:, rows],
                               cw_ref.at[w * ROUTE_ROWS:(w + 1) * ROUTE_ROWS, :])


def _ln1_route_sort_window(pre, l1g_ref, l1b_ref, wrh_ref, wrl_ref, br_ref, x1_ref, xs_ref, rg_ref, cw_ref):
    ts = pre.shape[0]
    x1 = _ln(pre, l1g_ref[...], l1b_ref[...])
    x1_ref[...] = x1

    xh = x1.astype(BF16)
    xl = (x1 - xh.astype(F32)).astype(BF16)
    wrh = wrh_ref[...]
    logits = (lax.dot_general(wrh, xh, _NT, preferred_element_type=F32)
              + lax.dot_general(wrh, xl, _NT, preferred_element_type=F32)
              + lax.dot_general(wrl_ref[...], xh, _NT, preferred_element_type=F32)) + br_ref[...]

    row = lax.broadcasted_iota(I32, logits.shape, 0).astype(F32)
    neg = jnp.float32(-jnp.inf)
    no_row = jnp.float32(ROUTE_ROWS)
    is_grp = row < N_GROUPS
    gl = jnp.where(is_grp, logits, neg)
    gm = jnp.max(gl, axis=0, keepdims=True)
    gidx = jnp.min(jnp.where(gl == gm, row, no_row), axis=0, keepdims=True)
    denom = jnp.sum(jnp.where(is_grp, jnp.exp(gl - gm), 0.0), axis=0, keepdims=True)
    p_g = 1.0 / denom

    lo = ROUTE_ROW0 + EXPERTS_PER_GROUP * gidx
    sel = (row >= lo) & (row < lo + EXPERTS_PER_GROUP)
    sl = jnp.where(sel, logits, neg)
    m1 = jnp.max(sl, axis=0, keepdims=True)
    i1 = jnp.min(jnp.where(sl == m1, row, no_row), axis=0, keepdims=True)
    sl2 = jnp.where(row == i1, neg, sl)
    m2 = jnp.max(sl2, axis=0, keepdims=True)
    i2 = jnp.min(jnp.where(sl2 == m2, row, no_row), axis=0, keepdims=True)
    e21 = jnp.exp(m2 - m1)
    g0 = p_g / (1.0 + e21)
    g1 = p_g * e21 / (1.0 + e21)

    oh0 = row == i1
    oh1 = row == i2
    s_f = jnp.where(oh0 | oh1, 1.0, 0.0)
    rr = lax.broadcasted_iota(I32, (ts, ts), 0)
    cc = lax.broadcasted_iota(I32, (ts, ts), 1)
    earlier = jnp.where(rr < cc, 1.0, 0.0).astype(BF16)
    rank = jnp.dot(s_f.astype(BF16), earlier, preferred_element_type=F32)
    cnt = jnp.sum(s_f, axis=1, keepdims=True)
    run = jnp.ceil(cnt * (1.0 / SUBLANES)) * SUBLANES
    er = lax.broadcasted_iota(I32, (ROUTE_ROWS, ROUTE_ROWS), 0)
    ec = lax.broadcasted_iota(I32, (ROUTE_ROWS, ROUTE_ROWS), 1)
    lower = jnp.where(ec < er, 1.0, 0.0).astype(BF16)
    run_start = jnp.dot(lower, jnp.broadcast_to(run, (ROUTE_ROWS, LANES)).astype(BF16),
                        preferred_element_type=F32)[:, 0:1]
    where_to = rank + run_start
    lp0 = jnp.sum(jnp.where(oh0, where_to, 0.0), axis=0, keepdims=True)
    lp1 = jnp.sum(jnp.where(oh1, where_to, 0.0), axis=0, keepdims=True)

    srow = lax.broadcasted_iota(I32, (WIN_ROWS, ts), 0).astype(F32)
    place = jnp.where((srow == lp0) | (srow == lp1), 1.0, 0.0).astype(BF16)
    xs_ref[...] = jnp.dot(place, xh, preferred_element_type=F32)

    r8 = lax.broadcasted_iota(I32, (SUBLANES, ts), 0)
    rg_ref[...] = jnp.where(r8 == 0, g0, jnp.where(r8 == 1, g1, jnp.where(r8 == 2, lp0,
                            jnp.where(r8 == 3, lp1, 0.0))))
    cw_ref[...] = jnp.broadcast_to(cnt, cw_ref.shape)


def _tail_out_shapes(t, d):
    nw = t // TS_TAIL
    return (jax.ShapeDtypeStruct((t, d), F32),
            jax.ShapeDtypeStruct((nw * WIN_ROWS, d), F32),
            jax.ShapeDtypeStruct((SUBLANES, t), F32),
            jax.ShapeDtypeStruct((nw * ROUTE_ROWS, SUBLANES), F32))


def _tail_out_specs(d, tok, tok_t):
    return (pl.BlockSpec((TS_STEP, d), tok),
            pl.BlockSpec((TAIL_WINDOWS * WIN_ROWS, d), tok),
            pl.BlockSpec((SUBLANES, TS_STEP), tok_t),
            pl.BlockSpec((TAIL_WINDOWS * ROUTE_ROWS, SUBLANES), tok))


def _pw1_glu_kernel(x_ref, w_ref, b_ref, o_ref):
    d = o_ref.shape[-1]
    h = jnp.dot(x_ref[...].astype(BF16), w_ref[...], preferred_element_type=F32) + b_ref[...]
    o_ref[...] = h[:, :d] * jax.nn.sigmoid(h[:, d:])


def _pw1_glu(x, w_bf, b):
    t, d = x.shape
    tm = TM_PW1
    return pl.pallas_call(
        _pw1_glu_kernel,
        grid=(t // tm,),
        in_specs=[pl.BlockSpec((tm, d), lambda i: (i, 0)),
                  pl.BlockSpec((d, 2 * d), lambda i: (0, 0)),
                  pl.BlockSpec((1, 2 * d), lambda i: (0, 0))],
        out_specs=pl.BlockSpec((tm, d), lambda i: (i, 0)),
        out_shape=jax.ShapeDtypeStruct((t, d), F32),
        compiler_params=_cparams(("arbitrary",)),
        name="conv_pw1_glu",
    )(x, w_bf, b.reshape(1, -1))


def _conv_tail_kernel(hcur_ref, hprev_ref, x_ref, wdw_ref, bdw_ref, lng_ref, lnb_ref, wpw2_ref, bpw2_ref,
                      l1g_ref, l1b_ref, wrh_ref, wrl_ref, br_ref,
                      x1_ref, xs_ref, rg_ref, cw_ref,
                      hext_ref, hsh_ref, conv_ref):
    ts, d = hcur_ref.shape
    j = pl.program_id(1)

    hext_ref[HALO:, :] = hcur_ref[...]

    @pl.when(j == 0)
    def _():
        hext_ref[0:HALO, :] = jnp.zeros((HALO, d), F32)

    @pl.when(j > 0)
    def _():
        hext_ref[0:HALO, :] = hprev_ref[...]

    n_sh = ts + HALO - SUBLANES
    offs = [HALO - (CONV_WIDTH - 1) + k for k in range(CONV_WIDTH)]

    for c in range(d // LANES):
        lanes = slice(c * LANES, (c + 1) * LANES)
        for b in range(1, SUBLANES):
            hsh_ref[b - 1] = hext_ref[b:b + n_sh, lanes]
        w_rows = [jnp.broadcast_to(wdw_ref[k:k + 1, lanes], (SUBLANES, LANES)) for k in range(CONV_WIDTH)]
        bias = jnp.broadcast_to(bdw_ref[:, lanes], (SUBLANES, LANES))

        def conv_rows(r, carry, lanes=lanes, w_rows=w_rows, bias=bias):
            r0 = pl.multiple_of(r * CONV_ROWS, CONV_ROWS)
            for grp in range(CONV_ROWS // SUBLANES):
                acc = bias
                for k, off in enumerate(offs):
                    start = r0 + (grp + off // SUBLANES) * SUBLANES
                    if off % SUBLANES == 0:
                        tap = hext_ref[pl.ds(start, SUBLANES), lanes]
                    else:
                        tap = hsh_ref[off % SUBLANES - 1, pl.ds(start, SUBLANES), :]
                    acc = acc + w_rows[k] * tap
                conv_ref[pl.ds(r0 + grp * SUBLANES, SUBLANES), lanes] = acc
            return carry

        lax.fori_loop(0, ts // CONV_ROWS, conv_rows, 0)

    hn = _silu(_ln(conv_ref[...], lng_ref[...], lnb_ref[...]))
    mix = jnp.dot(hn.astype(BF16), wpw2_ref[...], preferred_element_type=F32) + bpw2_ref[...]
    pre = DEEPNORM_ALPHA * x_ref[...] + mix
    _ln1_route_sort(pre, l1g_ref, l1b_ref, wrh_ref, wrl_ref, br_ref, x1_ref, xs_ref, rg_ref, cw_ref)


def _conv_tail(h, x, batch, seq, w_dw, b_dw, ln_g, ln_b, w_pw2_bf, b_pw2, l1g, l1b, wrh, wrl, br):
    t, d = x.shape
    ts = TS_STEP
    nj = seq // ts
    halo_per_tile = ts // HALO
    h3 = h.reshape(batch, seq, d)
    x3 = x.reshape(batch, seq, d)
    row = lambda v: v.reshape(1, -1)
    const2 = lambda shape: pl.BlockSpec(shape, lambda b, j: (0, 0))
    tok = lambda b, j: (b * nj + j, 0)
    tok_t = lambda b, j: (0, b * nj + j)
    return pl.pallas_call(
        _conv_tail_kernel,
        grid=(batch, nj),
        in_specs=[pl.BlockSpec((None, ts, d), lambda b, j: (b, j, 0)),
                  pl.BlockSpec((None, HALO, d), lambda b, j: (b, jnp.maximum(j * halo_per_tile - 1, 0), 0)),
                  pl.BlockSpec((None, ts, d), lambda b, j: (b, j, 0)),
                  const2((CONV_WIDTH, d)), const2((1, d)), const2((1, d)), const2((1, d)),
                  const2((d, d)), const2((1, d)), const2((1, d)), const2((1, d)),
                  const2((ROUTE_ROWS, d)), const2((ROUTE_ROWS, d)), const2((ROUTE_ROWS, 1))],
        out_specs=_tail_out_specs(d, tok, tok_t),
        out_shape=_tail_out_shapes(t, d),
        scratch_shapes=[pltpu.VMEM((ts + HALO, d), F32), pltpu.VMEM((SUBLANES - 1, ts + HALO - SUBLANES, LANES), F32),
                        pltpu.VMEM((ts, d), F32)],
        compiler_params=_cparams(("arbitrary", "arbitrary")),
        name="conv_tail_ln1_route",
    )(h3, h3, x3, w_dw, row(b_dw), row(ln_g), row(ln_b), w_pw2_bf, row(b_pw2), row(l1g), row(l1b), wrh, wrl, br)


def _rope_table_kernel(pos_ref, invf_ref, cos_ref, sin_ref):
    ang = pos_ref[...].astype(F32) * invf_ref[...]
    cos_ref[...] = jnp.cos(ang)
    sin_ref[...] = jnp.sin(ang)


def _rope_tables(positions, half):
    t = positions.size
    tm = 1024
    inv_freq = ROPE_BASE ** (-jnp.arange(half, dtype=F32) / half)
    return pl.pallas_call(
        _rope_table_kernel,
        grid=(t // tm,),
        in_specs=[pl.BlockSpec((tm, 1), lambda i: (i, 0)), pl.BlockSpec((1, half), lambda i: (0, 0))],
        out_specs=(pl.BlockSpec((tm, half), lambda i: (i, 0)), pl.BlockSpec((tm, half), lambda i: (i, 0))),
        out_shape=(jax.ShapeDtypeStruct((t, half), F32), jax.ShapeDtypeStruct((t, half), F32)),
        compiler_params=_cparams(("arbitrary",)),
        name="rope_tables",
    )(positions.reshape(t, 1), inv_freq.reshape(1, half))


def _rotate(t, cos, sin, head_dim):
    half = head_dim // 2
    parts = []
    for h in range(t.shape[1] // head_dim):
        t1 = t[:, h * head_dim:h * head_dim + half]
        t2 = t[:, h * head_dim + half:(h + 1) * head_dim]
        parts.append(t1 * cos - t2 * sin)
        parts.append(t1 * sin + t2 * cos)
    return jnp.concatenate(parts, axis=-1)


def _qkvg_kernel(x_ref, wq_ref, wk_ref, wv_ref, wg_ref, cos_ref, sin_ref, q_ref, k_ref, v_ref, g_ref, *, head_qk):
    xb = x_ref[...].astype(BF16)
    cos = cos_ref[...]
    sin = sin_ref[...]
    q = jnp.dot(xb, wq_ref[...], preferred_element_type=F32)
    q_ref[...] = _rotate(q, cos, sin, head_qk).astype(BF16)
    k = jnp.dot(xb, wk_ref[...], preferred_element_type=F32)
    k_ref[...] = (_rotate(k, cos, sin, head_qk) * (head_qk ** -0.5)).astype(BF16)
    v_ref[...] = jnp.dot(xb, wv_ref[...], preferred_element_type=F32).astype(BF16)
    g_ref[...] = jnp.dot(xb, wg_ref[...], preferred_element_type=F32).astype(BF16)


def _qkvg(x, w_bf, qk, vd, cos, sin):
    t, d = x.shape
    head_qk = qk // RET_HEADS
    tm = TM_QKVG
    assert vd == 2 * qk
    tokb = lambda n: pl.BlockSpec((tm, n), lambda i: (i, 0))
    wcol = lambda n, j: pl.BlockSpec((d, n), lambda i: (0, j))
    return pl.pallas_call(
        functools.partial(_qkvg_kernel, head_qk=head_qk),
        grid=(t // tm,),
        in_specs=[tokb(d), wcol(qk, 0), wcol(qk, 1), wcol(vd, 1), wcol(vd, 2), tokb(head_qk // 2), tokb(head_qk // 2)],
        out_specs=(tokb(qk), tokb(qk), tokb(vd), tokb(vd)),
        out_shape=(jax.ShapeDtypeStruct((t, qk), BF16), jax.ShapeDtypeStruct((t, qk), BF16),
                   jax.ShapeDtypeStruct((t, vd), BF16), jax.ShapeDtypeStruct((t, vd), BF16)),
        compiler_params=_cparams(("arbitrary",)),
        name="ret_qkvg_rope",
    )(x, w_bf, w_bf, w_bf, w_bf, cos, sin)


def _ret_core_kernel(q_ref, k_ref, v_ref, g_ref, mask_ref, xi_ref, zeta_ref, dec_ref, gng_ref, gnb_ref,
                     y_ref, state_ref):
    blk = RET_BLOCK
    n_blk = q_ref.shape[0] // blk

    @pl.when(pl.program_id(2) == 0)
    def _():
        state_ref[...] = jnp.zeros_like(state_ref)

    mask = mask_ref[...]
    xi = xi_ref[...]
    zeta = zeta_ref[...]
    dec = dec_ref[0:1, 0:1]
    for n in range(n_blk):
        rows = slice(n * blk, (n + 1) * blk)
        q = q_ref[rows, :]
        k = k_ref[rows, :]
        v = v_ref[rows, :]
        state = state_ref[...]
        s = lax.dot_general(q, k, _NT, preferred_element_type=F32)
        p = (s * mask).astype(BF16)
        o = jnp.dot(p, v, preferred_element_type=F32)
        qx = (q.astype(F32) * xi).astype(BF16)
        o = o + jnp.dot(qx, state.astype(BF16), preferred_element_type=F32)
        kz = (k.astype(F32) * zeta).astype(BF16)
        state_ref[...] = state * dec + lax.dot_general(kz, v, _TN, preferred_element_type=F32)
        mu = jnp.mean(o, axis=-1, keepdims=True)
        oc = o - mu
        var = jnp.mean(oc * oc, axis=-1, keepdims=True)
        on = oc * lax.rsqrt(var + LN_EPS) * gng_ref[...] + gnb_ref[...]
        y_ref[rows, :] = (_silu(g_ref[rows, :].astype(F32)) * on).astype(BF16)


def _ret_tables():
    blk = RET_BLOCK
    log_gamma = jnp.log(1.0 - 2.0 ** (-5.0 - jnp.arange(RET_HEADS, dtype=F32)))
    idx = jnp.arange(blk, dtype=F32)
    dist = jnp.abs(idx[:, None] - idx[None, :])
    visible = (jnp.floor(idx[None, :] / RET_CHUNK) <= jnp.floor(idx[:, None] / RET_CHUNK))
    mask = jnp.where(visible[None], jnp.exp(log_gamma[:, None, None] * dist[None]), 0.0)
    xi = jnp.exp(log_gamma[:, None] * (idx[None, :] + 1.0))[..., None]
    zeta = jnp.exp(log_gamma[:, None] * (blk - 1.0 - idx[None, :]))[..., None]
    dec = jnp.broadcast_to(jnp.exp(log_gamma * blk)[:, None, None], (RET_HEADS, SUBLANES, LANES))
    return mask.astype(F32), xi.astype(F32), zeta.astype(F32), dec.astype(F32)


def _ret_core(q, k, v, g, gn_g, gn_b, batch, seq):
    t, qk = q.shape
    vd = v.shape[1]
    hq = qk // RET_HEADS
    hv = vd // RET_HEADS
    sb = RET_SUPER
    ns = seq // sb
    mask, xi, zeta, dec = _ret_tables()
    tokb = lambda n: pl.BlockSpec((sb, n), lambda b, h, s: (b * ns + s, h))
    headb = lambda r, c: pl.BlockSpec((None, r, c), lambda b, h, s: (h, 0, 0))
    return pl.pallas_call(
        _ret_core_kernel,
        grid=(batch, RET_HEADS, ns),
        in_specs=[tokb(hq), tokb(hq), tokb(hv), tokb(hv),
                  headb(RET_BLOCK, RET_BLOCK), headb(RET_BLOCK, 1), headb(RET_BLOCK, 1), headb(SUBLANES, LANES),
                  pl.BlockSpec((1, hv), lambda b, h, s: (0, h)), pl.BlockSpec((1, hv), lambda b, h, s: (0, h))],
        out_specs=tokb(hv),
        out_shape=jax.ShapeDtypeStruct((t, vd), BF16),
        scratch_shapes=[pltpu.VMEM((hq, hv), F32)],
        compiler_params=_cparams(("arbitrary", "arbitrary", "arbitrary")),
        name="ret_core",
    )(q, k, v, g, mask, xi, zeta, dec, gn_g.reshape(1, -1), gn_b.reshape(1, -1))


def _ret_tail_kernel(y_ref, x_ref, wo_ref, l1g_ref, l1b_ref, wrh_ref, wrl_ref, br_ref,
                     x1_ref, xs_ref, rg_ref, cw_ref):
    mix = jnp.dot(y_ref[...], wo_ref[...], preferred_element_type=F32)
    pre = DEEPNORM_ALPHA * x_ref[...] + mix
    _ln1_route_sort(pre, l1g_ref, l1b_ref, wrh_ref, wrl_ref, br_ref, x1_ref, xs_ref, rg_ref, cw_ref)


def _ret_tail(y, x, w_o_bf, l1g, l1b, wrh, wrl, br):
    t, d = x.shape
    vd = y.shape[1]
    ts = TS_STEP
    row = lambda v: v.reshape(1, -1)
    const = lambda shape: pl.BlockSpec(shape, lambda i: (0, 0))
    tok = lambda i: (i, 0)
    tok_t = lambda i: (0, i)
    return pl.pallas_call(
        _ret_tail_kernel,
        grid=(t // ts,),
        in_specs=[pl.BlockSpec((ts, vd), tok), pl.BlockSpec((ts, d), tok), const((vd, d)),
                  const((1, d)), const((1, d)), const((ROUTE_ROWS, d)), const((ROUTE_ROWS, d)), const((ROUTE_ROWS, 1))],
        out_specs=_tail_out_specs(d, tok, tok_t),
        out_shape=_tail_out_shapes(t, d),
        compiler_params=_cparams(("arbitrary",)),
        name="ret_tail_ln1_route",
    )(y, x, w_o_bf, row(l1g), row(l1b), wrh, wrl, br)


def _gather_groups(src_hbm, table_ref, first, n_groups, dst_ref, sem):
    def body(jj, carry):
        for par in range(2):
            j = 2 * jj + par
            pltpu.make_async_copy(_row_group(src_hbm, table_ref[first + j]),
                                  _row_group(dst_ref, j * SUBLANES), sem).start(priority=par)
        return carry

    lax.fori_loop(0, n_groups // 2, body, 0, unroll=4)


def _wait_groups(src_hbm, dst_ref, sem):
    pltpu.make_async_copy(src_hbm.at[pl.ds(0, dst_ref.shape[0])], dst_ref, sem).wait()


def _ffn_kernel(gsrc_ref, tile_e_ref, tile_nv_ref, nt_ref, xs_hbm, wg_ref, wu_ref, wd_ref, ys_hbm,
                xbuf_ref, obuf_ref, wgb_ref, wub_ref, wdb_ref, sem_in, sem_out):
    del xs_hbm
    i = pl.program_id(0)
    n_steps = pl.num_programs(0)
    nt = nt_ref[0]
    slot = i % 2

    def scatter(tile, buf, sem, wait):
        def body(j, carry):
            cp = pltpu.make_async_copy(_row_group(buf, j * SUBLANES),
                                       _row_group(ys_hbm, gsrc_ref[tile * TILE_GROUPS + j]), sem)
            if wait:
                cp.wait()
            else:
                cp.start()
            return carry

        lax.fori_loop(0, tile_nv_ref[tile], body, 0)

    @pl.when(i == 0)
    def _():
        _gather_groups(ys_hbm, gsrc_ref, 0, TILE_GROUPS, xbuf_ref.at[0], sem_in.at[0])

    @pl.when((i >= 2) & (i - 2 < nt))
    def _():
        scatter(i - 2, obuf_ref.at[slot], sem_out.at[slot], wait=True)

    @pl.when(i < nt)
    def _():
        @pl.when((i == 0) | (tile_e_ref[i] != tile_e_ref[jnp.maximum(i - 1, 0)]))
        def _():
            wgb_ref[...] = wg_ref[...].astype(BF16)
            wub_ref[...] = wu_ref[...].astype(BF16)
            wdb_ref[...] = wd_ref[...].astype(BF16)

        _wait_groups(ys_hbm, xbuf_ref.at[slot], sem_in.at[slot])

        @pl.when(i + 1 < nt)
        def _():
            _gather_groups(ys_hbm, gsrc_ref, (i + 1) * TILE_GROUPS, TILE_GROUPS,
                           xbuf_ref.at[1 - slot], sem_in.at[1 - slot])

        x = xbuf_ref[slot].astype(BF16)
        a = jnp.dot(x, wgb_ref[...], preferred_element_type=F32)
        u = jnp.dot(x, wub_ref[...], preferred_element_type=F32)
        h = (_silu(a) * u).astype(BF16)
        obuf_ref[slot] = jnp.dot(h, wdb_ref[...], preferred_element_type=F32)
        scatter(i, obuf_ref.at[slot], sem_out.at[slot], wait=False)

    @pl.when((i == n_steps - 1) & (i >= 1) & (i - 1 < nt))
    def _():
        scatter(i - 1, obuf_ref.at[1 - slot], sem_out.at[1 - slot], wait=True)

    @pl.when((i == n_steps - 1) & (i < nt))
    def _():
        scatter(i, obuf_ref.at[slot], sem_out.at[slot], wait=True)


def _ffn(xs, gsrc, tile_e, tile_nv, n_tiles, w_gate, w_up, w_down, layer, nt_max):
    d = xs.shape[1]
    f = w_gate.shape[-1]
    tm = TM_FFN
    wspec = lambda r, c: pl.BlockSpec((None, None, r, c), lambda i, gs, te, nv, nt: (layer, te[i], 0, 0))
    grid_spec = pltpu.PrefetchScalarGridSpec(
        num_scalar_prefetch=4,
        grid=(nt_max,),
        in_specs=[pl.BlockSpec(memory_space=pl.ANY), wspec(d, f), wspec(d, f), wspec(f, d)],
        out_specs=pl.BlockSpec(memory_space=pl.ANY),
        scratch_shapes=[pltpu.VMEM((2, tm, d), F32), pltpu.VMEM((2, tm, d), F32),
                        pltpu.VMEM((d, f), BF16), pltpu.VMEM((d, f), BF16), pltpu.VMEM((f, d), BF16),
                        pltpu.SemaphoreType.DMA((2,)), pltpu.SemaphoreType.DMA((2,))],
    )
    return pl.pallas_call(
        _ffn_kernel,
        grid_spec=grid_spec,
        out_shape=jax.ShapeDtypeStruct(xs.shape, F32),
        input_output_aliases={4: 0},
        compiler_params=_cparams(("arbitrary",)),
        name="moe_expert_ffn",
    )(gsrc, tile_e, tile_nv, n_tiles, xs, w_gate, w_up, w_down)


def _combine_ln2_kernel(x1_ref, rg_ref, ys_ref, l2g_ref, l2b_ref, o_ref):
    info = rg_ref[...]
    g0, g1, lp0, lp1 = info[0:1, :], info[1:2, :], info[2:3, :], info[3:4, :]
    ts = info.shape[1]
    srow = lax.broadcasted_iota(I32, (WIN_ROWS, ts), 0).astype(F32)
    unsort = (jnp.where(srow == lp0, g0, 0.0) + jnp.where(srow == lp1, g1, 0.0)).astype(BF16)
    ffn = lax.dot_general(unsort, ys_ref[...].astype(BF16), _TN, preferred_element_type=F32)
    o_ref[...] = _ln(DEEPNORM_ALPHA * x1_ref[...] + ffn, l2g_ref[...], l2b_ref[...])


def _combine_ln2(x1, rg, ys, l2g, l2b):
    t, d = x1.shape
    ts = TS_TAIL
    return pl.pallas_call(
        _combine_ln2_kernel,
        grid=(t // ts,),
        in_specs=[pl.BlockSpec((ts, d), lambda i: (i, 0)),
                  pl.BlockSpec((SUBLANES, ts), lambda i: (0, i)),
                  pl.BlockSpec((WIN_ROWS, d), lambda i: (i, 0)),
                  pl.BlockSpec((1, d), lambda i: (0, 0)),
                  pl.BlockSpec((1, d), lambda i: (0, 0))],
        out_specs=pl.BlockSpec((ts, d), lambda i: (i, 0)),
        out_shape=jax.ShapeDtypeStruct((t, d), F32),
        compiler_params=_cparams(("arbitrary",)),
        name="moe_combine_ln2",
    )(x1, rg, ys, l2g.reshape(1, -1), l2b.reshape(1, -1))


def _pick_last(below, values, axis):
    first = lax.index_in_dim(values, 0, axis, keepdims=False)
    steps = lax.slice_in_dim(values, 1, None, axis=axis) - lax.slice_in_dim(values, 0, -1, axis=axis)
    return first + jnp.sum(jnp.where(below, steps, 0), axis=axis)


def _moe_tables(cw, nw, nt_max):
    tm = TM_FFN
    cnt = cw.reshape(nw, ROUTE_ROWS, SUBLANES)[:, ROUTE_ROW0:ROUTE_ROW0 + N_EXPERTS, 0].astype(I32)
    run = ((cnt + SUBLANES - 1) // SUBLANES) * SUBLANES
    loc = jnp.cumsum(run, axis=1) - run
    e_rows = jnp.sum(run, axis=0)
    e_pad = ((e_rows + tm - 1) // tm) * tm
    e_end = jnp.cumsum(e_pad)
    e_off = e_end - e_pad
    glob = e_off[None, :] + jnp.cumsum(run, axis=0) - run

    starts = glob.T.reshape(-1)
    ends = starts + run.T.reshape(-1)
    shift = (jnp.arange(nw, dtype=I32)[:, None] * WIN_ROWS + loc).T.reshape(-1) - starts
    rows = jnp.arange(nt_max * TILE_GROUPS, dtype=I32) * SUBLANES
    below = starts[None, 1:] <= rows[:, None]
    zero_group = WIN_ROWS - SUBLANES
    gsrc = jnp.where(rows < _pick_last(below, ends[None, :], 1), rows + _pick_last(below, shift[None, :], 1),
                     zero_group).astype(I32)
    tile_start = jnp.arange(nt_max, dtype=I32) * tm
    done = e_end[None, :-1] <= tile_start[:, None]
    tile_e = jnp.sum(done.astype(I32), axis=1)
    real_end = _pick_last(done, (e_off + e_rows)[None, :], 1)
    tile_nv = jnp.clip(real_end - tile_start, 0, tm) // SUBLANES
    n_tiles = (e_end[-1:] // tm).astype(I32)
    return gsrc, tile_e.astype(I32), tile_nv.astype(I32), n_tiles


def _moe(x1, xs, rg, cw, w_gate, w_up, w_down, layer, l2g, l2b):
    t = x1.shape[0]
    nw = t // TS_TAIL
    max_rows = TOP_K * t + nw * N_EXPERTS * (SUBLANES - 1) + N_EXPERTS * (TM_FFN - 1)
    nt_max = -(-max_rows // TM_FFN)
    gsrc, tile_e, tile_nv, n_tiles = _moe_tables(cw, nw, nt_max)
    ys = _ffn(xs, gsrc, tile_e, tile_nv, n_tiles, w_gate, w_up, w_down, layer, nt_max)
    return _combine_ln2(x1, rg, ys, l2g, l2b)


def _router_weights(w_grp, b_grp, w_route, b_route):
    d = w_grp.shape[0]
    used = N_GROUPS + N_EXPERTS
    w = jnp.concatenate([w_grp, w_route, jnp.zeros((d, ROUTE_ROWS - used), F32)], axis=1).T
    b = jnp.concatenate([b_grp, b_route, jnp.zeros((ROUTE_ROWS - used,), F32)]).reshape(ROUTE_ROWS, 1)
    wh = w.astype(BF16)
    wl = (w - wh.astype(F32)).astype(BF16)
    return wh, wl, b


def kernel(x, positions, conv_w_pw1, conv_b_pw1, conv_w_dw, conv_b_dw, conv_ln_g, conv_ln_b, conv_w_pw2, conv_b_pw2,
           ret_w_qkvg, ret_gn_g, ret_gn_b, ret_w_o, ln1_g, ln1_b, ln2_g, ln2_b, moe_w_grp, moe_b_grp, moe_w_route,
           moe_b_route, moe_w_gate, moe_w_up, moe_w_down):
    batch, seq, d = x.shape
    t = batch * seq
    qk = d
    vd = 2 * d
    xt = x.reshape(t, d)
    cos, sin = _rope_tables(positions, qk // RET_HEADS // 2)
    for i in range(DEPTH):
        j = i // N_MIXERS
        wrh, wrl, br = _router_weights(moe_w_grp[i], moe_b_grp[i], moe_w_route[i], moe_b_route[i])
        if i % N_MIXERS == 0:
            h = _pw1_glu(xt, conv_w_pw1[j].astype(BF16), conv_b_pw1[j])
            x1, xs, rg, cw = _conv_tail(h, xt, batch, seq, conv_w_dw[j], conv_b_dw[j], conv_ln_g[j], conv_ln_b[j],
                                        conv_w_pw2[j].astype(BF16), conv_b_pw2[j], ln1_g[i], ln1_b[i], wrh, wrl, br)
        else:
            q, k, v, g = _qkvg(xt, ret_w_qkvg[j].astype(BF16), qk, vd, cos, sin)
            y = _ret_core(q, k, v, g, ret_gn_g[j], ret_gn_b[j], batch, seq)
            x1, xs, rg, cw = _ret_tail(y, xt, ret_w_o[j].astype(BF16), ln1_g[i], ln1_b[i], wrh, wrl, br)
        xt = _moe(x1, xs, rg, cw, moe_w_gate, moe_w_up, moe_w_down, i, ln2_g[i], ln2_b[i])
    return xt.reshape(batch, seq, d)
```

```python
import functools

import jax
import jax.numpy as jnp
from jax import lax
from jax.experimental import pallas as pl
from jax.experimental.pallas import tpu as pltpu

F32 = jnp.float32
BF16 = jnp.bfloat16
I32 = jnp.int32

DEPTH = 4
N_MIXERS = 2
CONV_WIDTH = 31
RET_HEADS = 4
RET_CHUNK = 64
N_GROUPS = 4
EXPERTS_PER_GROUP = 8
N_EXPERTS = N_GROUPS * EXPERTS_PER_GROUP
TOP_K = 2
ROPE_BASE = 10000.0
DEEPNORM_ALPHA = (2.0 * DEPTH) ** 0.25
LN_EPS = 1e-5

LANES = 128
SUBLANES = 8
VMEM_LIMIT = 56 * 1024 * 1024

HALO = 32
CONV_ROWS = 64
TS_TAIL = 256
TAIL_WINDOWS = 2
TS_STEP = TAIL_WINDOWS * TS_TAIL
TM_PW1 = 512
TM_QKVG = 256
RET_BLOCK = 256
RET_SUPER = 1024
TM_FFN = 512
ROUTE_ROWS = LANES
ROUTE_ROW0 = N_GROUPS
WIN_ROWS = -(-(TOP_K * TS_TAIL + N_EXPERTS * (SUBLANES - 1) + SUBLANES) // LANES) * LANES
WIN_GROUPS = WIN_ROWS // SUBLANES
TILE_GROUPS = TM_FFN // SUBLANES

_NT = (((1,), (1,)), ((), ()))
_TN = (((0,), (0,)), ((), ()))


def _cparams(sem):
    return pltpu.CompilerParams(dimension_semantics=sem, vmem_limit_bytes=VMEM_LIMIT)


def _ln(x, g, b):
    mu = jnp.mean(x, axis=-1, keepdims=True)
    xc = x - mu
    var = jnp.mean(xc * xc, axis=-1, keepdims=True)
    return xc * lax.rsqrt(var + LN_EPS) * g + b


def _silu(x):
    return x * jax.nn.sigmoid(x)


def _row_group(ref, row):
    return ref.at[pl.ds(pl.multiple_of(row, SUBLANES), SUBLANES)]


_HI16 = -65536


def _pack_pairs(v):
    half = v.shape[1] // 2
    vb = v.astype(BF16).astype(F32)
    hi = pltpu.bitcast(vb[:, :half], I32) & jnp.int32(_HI16)
    lo = lax.shift_right_logical(pltpu.bitcast(vb[:, half:], I32), 16)
    return hi | lo


def _unpack_pairs(p):
    left = pltpu.bitcast(p & jnp.int32(_HI16), F32)
    right = pltpu.bitcast(lax.shift_left(p, 16), F32)
    return jnp.concatenate([left, right], axis=-1).astype(BF16)


def _ln1_route_sort(pre, l1g_ref, l1b_ref, wrh_ref, wrl_ref, br_ref, x1_ref, xs_ref, rg_ref, cw_ref):
    for w in range(pre.shape[0] // TS_TAIL):
        rows = slice(w * TS_TAIL, (w + 1) * TS_TAIL)
        _ln1_route_sort_window(pre[rows, :], l1g_ref, l1b_ref, wrh_ref, wrl_ref, br_ref, x1_ref.at[rows, :],
                               xs_ref.at[w * WIN_ROWS:(w + 1) * WIN_ROWS, :], rg_ref.at[:, rows],
                               cw_ref.at[w * ROUTE_ROWS:(w + 1) * ROUTE_ROWS, :])


def _ln1_route_sort_window(pre, l1g_ref, l1b_ref, wrh_ref, wrl_ref, br_ref, x1_ref, xs_ref, rg_ref, cw_ref):
    ts = pre.shape[0]
    x1 = _ln(pre, l1g_ref[...], l1b_ref[...])
    x1_ref[...] = x1

    xh = x1.astype(BF16)
    xl = (x1 - xh.astype(F32)).astype(BF16)
    wrh = wrh_ref[...]
    logits = (lax.dot_general(wrh, xh, _NT, preferred_element_type=F32)
              + lax.dot_general(wrh, xl, _NT, preferred_element_type=F32)
              + lax.dot_general(wrl_ref[...], xh, _NT, preferred_element_type=F32)) + br_ref[...]

    row = lax.broadcasted_iota(I32, logits.shape, 0).astype(F32)
    neg = jnp.float32(-jnp.inf)
    no_row = jnp.float32(ROUTE_ROWS)
    is_grp = row < N_GROUPS
    gl = jnp.where(is_grp, logits, neg)
    gm = jnp.max(gl, axis=0, keepdims=True)
    gidx = jnp.min(jnp.where(gl == gm, row, no_row), axis=0, keepdims=True)
    denom = jnp.sum(jnp.where(is_grp, jnp.exp(gl - gm), 0.0), axis=0, keepdims=True)
    p_g = 1.0 / denom

    lo = ROUTE_ROW0 + EXPERTS_PER_GROUP * gidx
    sel = (row >= lo) & (row < lo + EXPERTS_PER_GROUP)
    sl = jnp.where(sel, logits, neg)
    m1 = jnp.max(sl, axis=0, keepdims=True)
    i1 = jnp.min(jnp.where(sl == m1, row, no_row), axis=0, keepdims=True)
    sl2 = jnp.where(row == i1, neg, sl)
    m2 = jnp.max(sl2, axis=0, keepdims=True)
    i2 = jnp.min(jnp.where(sl2 == m2, row, no_row), axis=0, keepdims=True)
    e21 = jnp.exp(m2 - m1)
    g0 = p_g / (1.0 + e21)
    g1 = p_g * e21 / (1.0 + e21)

    oh0 = row == i1
    oh1 = row == i2
    s_f = jnp.where(oh0 | oh1, 1.0, 0.0)
    rr = lax.broadcasted_iota(I32, (ts, ts), 0)
    cc = lax.broadcasted_iota(I32, (ts, ts), 1)
    earlier = jnp.where(rr < cc, 1.0, 0.0).astype(BF16)
    rank = jnp.dot(s_f.astype(BF16), earlier, preferred_element_type=F32)
    cnt = jnp.sum(s_f, axis=1, keepdims=True)
    run = jnp.ceil(cnt * (1.0 / SUBLANES)) * SUBLANES
    er = lax.broadcasted_iota(I32, (ROUTE_ROWS, ROUTE_ROWS), 0)
    ec = lax.broadcasted_iota(I32, (ROUTE_ROWS, ROUTE_ROWS), 1)
    lower = jnp.where(ec < er, 1.0, 0.0).astype(BF16)
    run_start = jnp.dot(lower, jnp.broadcast_to(run, (ROUTE_ROWS, LANES)).astype(BF16),
                        preferred_element_type=F32)[:, 0:1]
    where_to = rank + run_start
    lp0 = jnp.sum(jnp.where(oh0, where_to, 0.0), axis=0, keepdims=True)
    lp1 = jnp.sum(jnp.where(oh1, where_to, 0.0), axis=0, keepdims=True)

    srow = lax.broadcasted_iota(I32, (WIN_ROWS, ts), 0).astype(F32)
    place = jnp.where((srow == lp0) | (srow == lp1), 1.0, 0.0).astype(BF16)
    xs_ref[...] = _pack_pairs(jnp.dot(place, xh, preferred_element_type=F32))

    r8 = lax.broadcasted_iota(I32, (SUBLANES, ts), 0)
    rg_ref[...] = jnp.where(r8 == 0, g0, jnp.where(r8 == 1, g1, jnp.where(r8 == 2, lp0,
                            jnp.where(r8 == 3, lp1, 0.0))))
    cw_ref[...] = jnp.broadcast_to(cnt, cw_ref.shape)


def _tail_out_shapes(t, d):
    nw = t // TS_TAIL
    return (jax.ShapeDtypeStruct((t, d), F32),
            jax.ShapeDtypeStruct((nw * WIN_ROWS, d // 2), I32),
            jax.ShapeDtypeStruct((SUBLANES, t), F32),
            jax.ShapeDtypeStruct((nw * ROUTE_ROWS, SUBLANES), F32))


def _tail_out_specs(d, tok, tok_t):
    return (pl.BlockSpec((TS_STEP, d), tok),
            pl.BlockSpec((TAIL_WINDOWS * WIN_ROWS, d // 2), tok),
            pl.BlockSpec((SUBLANES, TS_STEP), tok_t),
            pl.BlockSpec((TAIL_WINDOWS * ROUTE_ROWS, SUBLANES), tok))


def _pw1_glu_kernel(x_ref, w_ref, b_ref, o_ref):
    d = o_ref.shape[-1]
    h = jnp.dot(x_ref[...].astype(BF16), w_ref[...], preferred_element_type=F32) + b_ref[...]
    o_ref[...] = h[:, :d] * jax.nn.sigmoid(h[:, d:])


def _pw1_glu(x, w_bf, b):
    t, d = x.shape
    tm = TM_PW1
    return pl.pallas_call(
        _pw1_glu_kernel,
        grid=(t // tm,),
        in_specs=[pl.BlockSpec((tm, d), lambda i: (i, 0)),
                  pl.BlockSpec((d, 2 * d), lambda i: (0, 0)),
                  pl.BlockSpec((1, 2 * d), lambda i: (0, 0))],
        out_specs=pl.BlockSpec((tm, d), lambda i: (i, 0)),
        out_shape=jax.ShapeDtypeStruct((t, d), F32),
        compiler_params=_cparams(("arbitrary",)),
        name="conv_pw1_glu",
    )(x, w_bf, b.reshape(1, -1))


def _conv_tail_kernel(hcur_ref, hprev_ref, x_ref, wdw_ref, bdw_ref, lng_ref, lnb_ref, wpw2_ref, bpw2_ref,
                      l1g_ref, l1b_ref, wrh_ref, wrl_ref, br_ref,
                      x1_ref, xs_ref, rg_ref, cw_ref,
                      hext_ref, hsh_ref, conv_ref):
    ts, d = hcur_ref.shape
    j = pl.program_id(1)

    hext_ref[HALO:, :] = hcur_ref[...]

    @pl.when(j == 0)
    def _():
        hext_ref[0:HALO, :] = jnp.zeros((HALO, d), F32)

    @pl.when(j > 0)
    def _():
        hext_ref[0:HALO, :] = hprev_ref[...]

    n_sh = ts + HALO - SUBLANES
    offs = [HALO - (CONV_WIDTH - 1) + k for k in range(CONV_WIDTH)]

    for c in range(d // LANES):
        lanes = slice(c * LANES, (c + 1) * LANES)
        for b in range(1, SUBLANES):
            hsh_ref[b - 1] = hext_ref[b:b + n_sh, lanes]
        w_rows = [jnp.broadcast_to(wdw_ref[k:k + 1, lanes], (SUBLANES, LANES)) for k in range(CONV_WIDTH)]
        bias = jnp.broadcast_to(bdw_ref[:, lanes], (SUBLANES, LANES))

        def conv_rows(r, carry, lanes=lanes, w_rows=w_rows, bias=bias):
            r0 = pl.multiple_of(r * CONV_ROWS, CONV_ROWS)
            for grp in range(CONV_ROWS // SUBLANES):
                acc = bias
                for k, off in enumerate(offs):
                    start = r0 + (grp + off // SUBLANES) * SUBLANES
                    if off % SUBLANES == 0:
                        tap = hext_ref[pl.ds(start, SUBLANES), lanes]
                    else:
                        tap = hsh_ref[off % SUBLANES - 1, pl.ds(start, SUBLANES), :]
                    acc = acc + w_rows[k] * tap
                conv_ref[pl.ds(r0 + grp * SUBLANES, SUBLANES), lanes] = acc
            return carry

        lax.fori_loop(0, ts // CONV_ROWS, conv_rows, 0)

    hn = _silu(_ln(conv_ref[...], lng_ref[...], lnb_ref[...]))
    mix = jnp.dot(hn.astype(BF16), wpw2_ref[...], preferred_element_type=F32) + bpw2_ref[...]
    pre = DEEPNORM_ALPHA * x_ref[...] + mix
    _ln1_route_sort(pre, l1g_ref, l1b_ref, wrh_ref, wrl_ref, br_ref, x1_ref, xs_ref, rg_ref, cw_ref)


def _conv_tail(h, x, batch, seq, w_dw, b_dw, ln_g, ln_b, w_pw2_bf, b_pw2, l1g, l1b, wrh, wrl, br):
    t, d = x.shape
    ts = TS_STEP
    nj = seq // ts
    halo_per_tile = ts // HALO
    h3 = h.reshape(batch, seq, d)
    x3 = x.reshape(batch, seq, d)
    row = lambda v: v.reshape(1, -1)
    const2 = lambda shape: pl.BlockSpec(shape, lambda b, j: (0, 0))
    tok = lambda b, j: (b * nj + j, 0)
    tok_t = lambda b, j: (0, b * nj + j)
    return pl.pallas_call(
        _conv_tail_kernel,
        grid=(batch, nj),
        in_specs=[pl.BlockSpec((None, ts, d), lambda b, j: (b, j, 0)),
                  pl.BlockSpec((None, HALO, d), lambda b, j: (b, jnp.maximum(j * halo_per_tile - 1, 0), 0)),
                  pl.BlockSpec((None, ts, d), lambda b, j: (b, j, 0)),
                  const2((CONV_WIDTH, d)), const2((1, d)), const2((1, d)), const2((1, d)),
                  const2((d, d)), const2((1, d)), const2((1, d)), const2((1, d)),
                  const2((ROUTE_ROWS, d)), const2((ROUTE_ROWS, d)), const2((ROUTE_ROWS, 1))],
        out_specs=_tail_out_specs(d, tok, tok_t),
        out_shape=_tail_out_shapes(t, d),
        scratch_shapes=[pltpu.VMEM((ts + HALO, d), F32), pltpu.VMEM((SUBLANES - 1, ts + HALO - SUBLANES, LANES), F32),
                        pltpu.VMEM((ts, d), F32)],
        compiler_params=_cparams(("arbitrary", "arbitrary")),
        name="conv_tail_ln1_route",
    )(h3, h3, x3, w_dw, row(b_dw), row(ln_g), row(ln_b), w_pw2_bf, row(b_pw2), row(l1g), row(l1b), wrh, wrl, br)


def _rope_table_kernel(pos_ref, invf_ref, cos_ref, sin_ref):
    ang = pos_ref[...].astype(F32) * invf_ref[...]
    cos_ref[...] = jnp.cos(ang)
    sin_ref[...] = jnp.sin(ang)


def _rope_tables(positions, half):
    t = positions.size
    tm = 1024
    inv_freq = ROPE_BASE ** (-jnp.arange(half, dtype=F32) / half)
    return pl.pallas_call(
        _rope_table_kernel,
        grid=(t // tm,),
        in_specs=[pl.BlockSpec((tm, 1), lambda i: (i, 0)), pl.BlockSpec((1, half), lambda i: (0, 0))],
        out_specs=(pl.BlockSpec((tm, half), lambda i: (i, 0)), pl.BlockSpec((tm, half), lambda i: (i, 0))),
        out_shape=(jax.ShapeDtypeStruct((t, half), F32), jax.ShapeDtypeStruct((t, half), F32)),
        compiler_params=_cparams(("arbitrary",)),
        name="rope_tables",
    )(positions.reshape(t, 1), inv_freq.reshape(1, half))


def _rotate(t, cos, sin, head_dim):
    half = head_dim // 2
    parts = []
    for h in range(t.shape[1] // head_dim):
        t1 = t[:, h * head_dim:h * head_dim + half]
        t2 = t[:, h * head_dim + half:(h + 1) * head_dim]
        parts.append(t1 * cos - t2 * sin)
        parts.append(t1 * sin + t2 * cos)
    return jnp.concatenate(parts, axis=-1)


def _qkvg_kernel(x_ref, wq_ref, wk_ref, wv_ref, wg_ref, cos_ref, sin_ref, q_ref, k_ref, v_ref, g_ref, *, head_qk):
    xb = x_ref[...].astype(BF16)
    cos = cos_ref[...]
    sin = sin_ref[...]
    q = jnp.dot(xb, wq_ref[...], preferred_element_type=F32)
    q_ref[...] = _rotate(q, cos, sin, head_qk).astype(BF16)
    k = jnp.dot(xb, wk_ref[...], preferred_element_type=F32)
    k_ref[...] = (_rotate(k, cos, sin, head_qk) * (head_qk ** -0.5)).astype(BF16)
    v_ref[...] = jnp.dot(xb, wv_ref[...], preferred_element_type=F32).astype(BF16)
    g_ref[...] = jnp.dot(xb, wg_ref[...], preferred_element_type=F32).astype(BF16)


def _qkvg(x, w_bf, qk, vd, cos, sin):
    t, d = x.shape
    head_qk = qk // RET_HEADS
    tm = TM_QKVG
    assert vd == 2 * qk
    tokb = lambda n: pl.BlockSpec((tm, n), lambda i: (i, 0))
    wcol = lambda n, j: pl.BlockSpec((d, n), lambda i: (0, j))
    return pl.pallas_call(
        functools.partial(_qkvg_kernel, head_qk=head_qk),
        grid=(t // tm,),
        in_specs=[tokb(d), wcol(qk, 0), wcol(qk, 1), wcol(vd, 1), wcol(vd, 2), tokb(head_qk // 2), tokb(head_qk // 2)],
        out_specs=(tokb(qk), tokb(qk), tokb(vd), tokb(vd)),
        out_shape=(jax.ShapeDtypeStruct((t, qk), BF16), jax.ShapeDtypeStruct((t, qk), BF16),
                   jax.ShapeDtypeStruct((t, vd), BF16), jax.ShapeDtypeStruct((t, vd), BF16)),
        compiler_params=_cparams(("arbitrary",)),
        name="ret_qkvg_rope",
    )(x, w_bf, w_bf, w_bf, w_bf, cos, sin)


def _ret_core_kernel(q_ref, k_ref, v_ref, g_ref, mask_ref, xi_ref, zeta_ref, dec_ref, gng_ref, gnb_ref,
                     y_ref, state_ref):
    blk = RET_BLOCK
    n_blk = q_ref.shape[0] // blk

    @pl.when(pl.program_id(2) == 0)
    def _():
        state_ref[...] = jnp.zeros_like(state_ref)

    mask = mask_ref[...]
    xi = xi_ref[...]
    zeta = zeta_ref[...]
    dec = dec_ref[0:1, 0:1]
    for n in range(n_blk):
        rows = slice(n * blk, (n + 1) * blk)
        q = q_ref[rows, :]
        k = k_ref[rows, :]
        v = v_ref[rows, :]
        state = state_ref[...]
        s = lax.dot_general(q, k, _NT, preferred_element_type=F32)
        p = (s * mask).astype(BF16)
        o = jnp.dot(p, v, preferred_element_type=F32)
        qx = (q.astype(F32) * xi).astype(BF16)
        o = o + jnp.dot(qx, state.astype(BF16), preferred_element_type=F32)
        kz = (k.astype(F32) * zeta).astype(BF16)
        state_ref[...] = state * dec + lax.dot_general(kz, v, _TN, preferred_element_type=F32)
        mu = jnp.mean(o, axis=-1, keepdims=True)
        oc = o - mu
        var = jnp.mean(oc * oc, axis=-1, keepdims=True)
        on = oc * lax.rsqrt(var + LN_EPS) * gng_ref[...] + gnb_ref[...]
        y_ref[rows, :] = (_silu(g_ref[rows, :].astype(F32)) * on).astype(BF16)


def _ret_tables():
    blk = RET_BLOCK
    log_gamma = jnp.log(1.0 - 2.0 ** (-5.0 - jnp.arange(RET_HEADS, dtype=F32)))
    idx = jnp.arange(blk, dtype=F32)
    dist = jnp.abs(idx[:, None] - idx[None, :])
    visible = (jnp.floor(idx[None, :] / RET_CHUNK) <= jnp.floor(idx[:, None] / RET_CHUNK))
    mask = jnp.where(visible[None], jnp.exp(log_gamma[:, None, None] * dist[None]), 0.0)
    xi = jnp.exp(log_gamma[:, None] * (idx[None, :] + 1.0))[..., None]
    zeta = jnp.exp(log_gamma[:, None] * (blk - 1.0 - idx[None, :]))[..., None]
    dec = jnp.broadcast_to(jnp.exp(log_gamma * blk)[:, None, None], (RET_HEADS, SUBLANES, LANES))
    return mask.astype(F32), xi.astype(F32), zeta.astype(F32), dec.astype(F32)


def _ret_core(q, k, v, g, gn_g, gn_b, batch, seq):
    t, qk = q.shape
    vd = v.shape[1]
    hq = qk // RET_HEADS
    hv = vd // RET_HEADS
    sb = RET_SUPER
    ns = seq // sb
    mask, xi, zeta, dec = _ret_tables()
    tokb = lambda n: pl.BlockSpec((sb, n), lambda b, h, s: (b * ns + s, h))
    headb = lambda r, c: pl.BlockSpec((None, r, c), lambda b, h, s: (h, 0, 0))
    return pl.pallas_call(
        _ret_core_kernel,
        grid=(batch, RET_HEADS, ns),
        in_specs=[tokb(hq), tokb(hq), tokb(hv), tokb(hv),
                  headb(RET_BLOCK, RET_BLOCK), headb(RET_BLOCK, 1), headb(RET_BLOCK, 1), headb(SUBLANES, LANES),
                  pl.BlockSpec((1, hv), lambda b, h, s: (0, h)), pl.BlockSpec((1, hv), lambda b, h, s: (0, h))],
        out_specs=tokb(hv),
        out_shape=jax.ShapeDtypeStruct((t, vd), BF16),
        scratch_shapes=[pltpu.VMEM((hq, hv), F32)],
        compiler_params=_cparams(("arbitrary", "arbitrary", "arbitrary")),
        name="ret_core",
    )(q, k, v, g, mask, xi, zeta, dec, gn_g.reshape(1, -1), gn_b.reshape(1, -1))


def _ret_tail_kernel(y_ref, x_ref, wo_ref, l1g_ref, l1b_ref, wrh_ref, wrl_ref, br_ref,
                     x1_ref, xs_ref, rg_ref, cw_ref):
    mix = jnp.dot(y_ref[...], wo_ref[...], preferred_element_type=F32)
    pre = DEEPNORM_ALPHA * x_ref[...] + mix
    _ln1_route_sort(pre, l1g_ref, l1b_ref, wrh_ref, wrl_ref, br_ref, x1_ref, xs_ref, rg_ref, cw_ref)


def _ret_tail(y, x, w_o_bf, l1g, l1b, wrh, wrl, br):
    t, d = x.shape
    vd = y.shape[1]
    ts = TS_STEP
    row = lambda v: v.reshape(1, -1)
    const = lambda shape: pl.BlockSpec(shape, lambda i: (0, 0))
    tok = lambda i: (i, 0)
    tok_t = lambda i: (0, i)
    return pl.pallas_call(
        _ret_tail_kernel,
        grid=(t // ts,),
        in_specs=[pl.BlockSpec((ts, vd), tok), pl.BlockSpec((ts, d), tok), const((vd, d)),
                  const((1, d)), const((1, d)), const((ROUTE_ROWS, d)), const((ROUTE_ROWS, d)), const((ROUTE_ROWS, 1))],
        out_specs=_tail_out_specs(d, tok, tok_t),
        out_shape=_tail_out_shapes(t, d),
        compiler_params=_cparams(("arbitrary",)),
        name="ret_tail_ln1_route",
    )(y, x, w_o_bf, row(l1g), row(l1b), wrh, wrl, br)


def _ffn_kernel(gsrc_ref, tile_e_ref, tile_nv_ref, nt_ref, xs_hbm, wg_ref, wu_ref, wd_ref, ys_hbm,
                xbuf_ref, obuf_ref, wgb_ref, wub_ref, wdb_ref, sem_in, sem_out):
    del xs_hbm
    i = pl.program_id(0)
    n_steps = pl.num_programs(0)
    nt = nt_ref[0]
    slot = i % 2

    def groups(tile, buf, sem, to_hbm, wait):
        count = tile_nv_ref[tile]

        def one(j, priority):
            in_hbm = _row_group(ys_hbm, gsrc_ref[tile * TILE_GROUPS + j])
            in_buf = _row_group(buf, j * SUBLANES)
            cp = pltpu.make_async_copy(in_buf, in_hbm, sem) if to_hbm else pltpu.make_async_copy(in_hbm, in_buf, sem)
            if wait:
                cp.wait()
            else:
                cp.start(priority=priority)

        def pair(jj, carry):
            one(2 * jj, 0)
            one(2 * jj + 1, 1)
            return carry

        lax.fori_loop(0, lax.shift_right_logical(count, 1), pair, 0)

        @pl.when((count & 1) == 1)
        def _():
            one(count - 1, 0)

    @pl.when(i == 0)
    def _():
        xbuf_ref[...] = jnp.zeros(xbuf_ref.shape, xbuf_ref.dtype)
        groups(0, xbuf_ref.at[0], sem_in.at[0], to_hbm=False, wait=False)

    @pl.when((i >= 2) & (i - 2 < nt))
    def _():
        groups(i - 2, obuf_ref.at[slot], sem_out.at[slot], to_hbm=True, wait=True)

    @pl.when(i < nt)
    def _():
        @pl.when((i == 0) | (tile_e_ref[i] != tile_e_ref[jnp.maximum(i - 1, 0)]))
        def _():
            wgb_ref[...] = wg_ref[...].astype(BF16)
            wub_ref[...] = wu_ref[...].astype(BF16)
            wdb_ref[...] = wd_ref[...].astype(BF16)

        groups(i, xbuf_ref.at[slot], sem_in.at[slot], to_hbm=False, wait=True)

        @pl.when(i + 1 < nt)
        def _():
            groups(i + 1, xbuf_ref.at[1 - slot], sem_in.at[1 - slot], to_hbm=False, wait=False)

        x = _unpack_pairs(xbuf_ref[slot])
        a = jnp.dot(x, wgb_ref[...], preferred_element_type=F32)
        u = jnp.dot(x, wub_ref[...], preferred_element_type=F32)
        h = (_silu(a) * u).astype(BF16)
        obuf_ref[slot] = _pack_pairs(jnp.dot(h, wdb_ref[...], preferred_element_type=F32))
        groups(i, obuf_ref.at[slot], sem_out.at[slot], to_hbm=True, wait=False)

    @pl.when((i == n_steps - 1) & (i >= 1) & (i - 1 < nt))
    def _():
        groups(i - 1, obuf_ref.at[1 - slot], sem_out.at[1 - slot], to_hbm=True, wait=True)

    @pl.when((i == n_steps - 1) & (i < nt))
    def _():
        groups(i, obuf_ref.at[slot], sem_out.at[slot], to_hbm=True, wait=True)


def _ffn(xs, gsrc, tile_e, tile_nv, n_tiles, w_gate, w_up, w_down, layer, nt_max):
    d, f = w_gate.shape[-2:]
    assert xs.shape[1] * 2 == d
    tm = TM_FFN
    wspec = lambda r, c: pl.BlockSpec((None, None, r, c), lambda i, gs, te, nv, nt: (layer, te[i], 0, 0))
    grid_spec = pltpu.PrefetchScalarGridSpec(
        num_scalar_prefetch=4,
        grid=(nt_max,),
        in_specs=[pl.BlockSpec(memory_space=pl.ANY), wspec(d, f), wspec(d, f), wspec(f, d)],
        out_specs=pl.BlockSpec(memory_space=pl.ANY),
        scratch_shapes=[pltpu.VMEM((2, tm, d // 2), I32), pltpu.VMEM((2, tm, d // 2), I32),
                        pltpu.VMEM((d, f), BF16), pltpu.VMEM((d, f), BF16), pltpu.VMEM((f, d), BF16),
                        pltpu.SemaphoreType.DMA((2,)), pltpu.SemaphoreType.DMA((2,))],
    )
    return pl.pallas_call(
        _ffn_kernel,
        grid_spec=grid_spec,
        out_shape=jax.ShapeDtypeStruct(xs.shape, xs.dtype),
        input_output_aliases={4: 0},
        compiler_params=_cparams(("arbitrary",)),
        name="moe_expert_ffn",
    )(gsrc, tile_e, tile_nv, n_tiles, xs, w_gate, w_up, w_down)


def _combine_ln2_kernel(x1_ref, rg_ref, ys_ref, l2g_ref, l2b_ref, o_ref):
    info = rg_ref[...]
    g0, g1, lp0, lp1 = info[0:1, :], info[1:2, :], info[2:3, :], info[3:4, :]
    ts = info.shape[1]
    srow = lax.broadcasted_iota(I32, (WIN_ROWS, ts), 0).astype(F32)
    unsort = (jnp.where(srow == lp0, g0, 0.0) + jnp.where(srow == lp1, g1, 0.0)).astype(BF16)
    ffn = lax.dot_general(unsort, _unpack_pairs(ys_ref[...]), _TN, preferred_element_type=F32)
    o_ref[...] = _ln(DEEPNORM_ALPHA * x1_ref[...] + ffn, l2g_ref[...], l2b_ref[...])


def _combine_ln2(x1, rg, ys, l2g, l2b):
    t, d = x1.shape
    ts = TS_TAIL
    return pl.pallas_call(
        _combine_ln2_kernel,
        grid=(t // ts,),
        in_specs=[pl.BlockSpec((ts, d), lambda i: (i, 0)),
                  pl.BlockSpec((SUBLANES, ts), lambda i: (0, i)),
                  pl.BlockSpec((WIN_ROWS, d // 2), lambda i: (i, 0)),
                  pl.BlockSpec((1, d), lambda i: (0, 0)),
                  pl.BlockSpec((1, d), lambda i: (0, 0))],
        out_specs=pl.BlockSpec((ts, d), lambda i: (i, 0)),
        out_shape=jax.ShapeDtypeStruct((t, d), F32),
        compiler_params=_cparams(("arbitrary",)),
        name="moe_combine_ln2",
    )(x1, rg, ys, l2g.reshape(1, -1), l2b.reshape(1, -1))


def _pick_last(below, values, axis):
    first = lax.index_in_dim(values, 0, axis, keepdims=False)
    steps = lax.slice_in_dim(values, 1, None, axis=axis) - lax.slice_in_dim(values, 0, -1, axis=axis)
    return first + jnp.sum(jnp.where(below, steps, 0), axis=axis)


def _moe_tables(cw, nw, nt_max):
    tm = TM_FFN
    cnt = cw.reshape(nw, ROUTE_ROWS, SUBLANES)[:, ROUTE_ROW0:ROUTE_ROW0 + N_EXPERTS, 0].astype(I32)
    run = ((cnt + SUBLANES - 1) // SUBLANES) * SUBLANES
    loc = jnp.cumsum(run, axis=1) - run
    e_rows = jnp.sum(run, axis=0)
    e_pad = ((e_rows + tm - 1) // tm) * tm
    e_end = jnp.cumsum(e_pad)
    e_off = e_end - e_pad
    glob = e_off[None, :] + jnp.cumsum(run, axis=0) - run

    starts = glob.T.reshape(-1)
    ends = starts + run.T.reshape(-1)
    shift = (jnp.arange(nw, dtype=I32)[:, None] * WIN_ROWS + loc).T.reshape(-1) - starts
    rows = jnp.arange(nt_max * TILE_GROUPS, dtype=I32) * SUBLANES
    below = starts[None, 1:] <= rows[:, None]
    zero_group = WIN_ROWS - SUBLANES
    gsrc = jnp.where(rows < _pick_last(below, ends[None, :], 1), rows + _pick_last(below, shift[None, :], 1),
                     zero_group).astype(I32)
    tile_start = jnp.arange(nt_max, dtype=I32) * tm
    done = e_end[None, :-1] <= tile_start[:, None]
    tile_e = jnp.sum(done.astype(I32), axis=1)
    real_end = _pick_last(done, (e_off + e_rows)[None, :], 1)
    tile_nv = jnp.clip(real_end - tile_start, 0, tm) // SUBLANES
    n_tiles = (e_end[-1:] // tm).astype(I32)
    return gsrc, tile_e.astype(I32), tile_nv.astype(I32), n_tiles


def _moe(x1, xs, rg, cw, w_gate, w_up, w_down, layer, l2g, l2b):
    t = x1.shape[0]
    nw = t // TS_TAIL
    max_rows = TOP_K * t + nw * N_EXPERTS * (SUBLANES - 1) + N_EXPERTS * (TM_FFN - 1)
    nt_max = -(-max_rows // TM_FFN)
    gsrc, tile_e, tile_nv, n_tiles = _moe_tables(cw, nw, nt_max)
    ys = _ffn(xs, gsrc, tile_e, tile_nv, n_tiles, w_gate, w_up, w_down, layer, nt_max)
    return _combine_ln2(x1, rg, ys, l2g, l2b)


def _router_weights(w_grp, b_grp, w_route, b_route):
    d = w_grp.shape[0]
    used = N_GROUPS + N_EXPERTS
    w = jnp.concatenate([w_grp, w_route, jnp.zeros((d, ROUTE_ROWS - used), F32)], axis=1).T
    b = jnp.concatenate([b_grp, b_route, jnp.zeros((ROUTE_ROWS - used,), F32)]).reshape(ROUTE_ROWS, 1)
    wh = w.astype(BF16)
    wl = (w - wh.astype(F32)).astype(BF16)
    return wh, wl, b


def kernel(x, positions, conv_w_pw1, conv_b_pw1, conv_w_dw, conv_b_dw, conv_ln_g, conv_ln_b, conv_w_pw2, conv_b_pw2,
           ret_w_qkvg, ret_gn_g, ret_gn_b, ret_w_o, ln1_g, ln1_b, ln2_g, ln2_b, moe_w_grp, moe_b_grp, moe_w_route,
           moe_b_route, moe_w_gate, moe_w_up, moe_w_down):
    batch, seq, d = x.shape
    t = batch * seq
    qk = d
    vd = 2 * d
    xt = x.reshape(t, d)
    cos, sin = _rope_tables(positions, qk // RET_HEADS // 2)
    for i in range(DEPTH):
        j = i // N_MIXERS
        wrh, wrl, br = _router_weights(moe_w_grp[i], moe_b_grp[i], moe_w_route[i], moe_b_route[i])
        if i % N_MIXERS == 0:
            h = _pw1_glu(xt, conv_w_pw1[j].astype(BF16), conv_b_pw1[j])
            x1, xs, rg, cw = _conv_tail(h, xt, batch, seq, conv_w_dw[j], conv_b_dw[j], conv_ln_g[j], conv_ln_b[j],
                                        conv_w_pw2[j].astype(BF16), conv_b_pw2[j], ln1_g[i], ln1_b[i], wrh, wrl, br)
        else:
            q, k, v, g = _qkvg(xt, ret_w_qkvg[j].astype(BF16), qk, vd, cos, sin)
            y = _ret_core(q, k, v, g, ret_gn_g[j], ret_gn_b[j], batch, seq)
            x1, xs, rg, cw = _ret_tail(y, xt, ret_w_o[j].astype(BF16), ln1_g[i], ln1_b[i], wrh, wrl, br)
        xt = _moe(x1, xs, rg, cw, moe_w_gate, moe_w_up, moe_w_down, i, ln2_g[i], ln2_b[i])
    return xt.reshape(batch, seq, d)
```

```python
import functools

import jax
import jax.numpy as jnp
from jax import lax
from jax.experimental import pallas as pl
from jax.experimental.pallas import tpu as pltpu

F32 = jnp.float32
BF16 = jnp.bfloat16
I32 = jnp.int32

DEPTH = 4
N_MIXERS = 2
CONV_WIDTH = 31
RET_HEADS = 4
RET_CHUNK = 64
N_GROUPS = 4
EXPERTS_PER_GROUP = 8
N_EXPERTS = N_GROUPS * EXPERTS_PER_GROUP
TOP_K = 2
ROPE_BASE = 10000.0
DEEPNORM_ALPHA = (2.0 * DEPTH) ** 0.25
LN_EPS = 1e-5

LANES = 128
SUBLANES = 8
VMEM_LIMIT = 56 * 1024 * 1024

HALO = 32
CONV_ROWS = 64
TS_TAIL = 256
TAIL_WINDOWS = 2
TS_STEP = TAIL_WINDOWS * TS_TAIL
TM_PW1 = 512
TM_QKVG = 256
RET_BLOCK = 256
RET_SUPER = 1024
TM_FFN = 512
ROUTE_ROWS = LANES
ROUTE_ROW0 = N_GROUPS
WIN_ROWS = -(-(TOP_K * TS_TAIL + N_EXPERTS * (SUBLANES - 1) + SUBLANES) // LANES) * LANES
WIN_GROUPS = WIN_ROWS // SUBLANES
TILE_GROUPS = TM_FFN // SUBLANES

_NT = (((1,), (1,)), ((), ()))
_TN = (((0,), (0,)), ((), ()))


def _cparams(sem):
    return pltpu.CompilerParams(dimension_semantics=sem, vmem_limit_bytes=VMEM_LIMIT)


def _ln(x, g, b):
    mu = jnp.mean(x, axis=-1, keepdims=True)
    xc = x - mu
    var = jnp.mean(xc * xc, axis=-1, keepdims=True)
    return xc * lax.rsqrt(var + LN_EPS) * g + b


def _silu(x):
    return x * jax.nn.sigmoid(x)


def _row_group(ref, row):
    return ref.at[pl.ds(pl.multiple_of(row, SUBLANES), SUBLANES)]


_HI16 = -65536


def _pack_pairs(v):
    half = v.shape[1] // 2
    vb = v.astype(BF16).astype(F32)
    hi = pltpu.bitcast(vb[:, :half], I32) & jnp.int32(_HI16)
    lo = lax.shift_right_logical(pltpu.bitcast(vb[:, half:], I32), 16)
    return hi | lo


def _unpack_pairs(p):
    left = pltpu.bitcast(p & jnp.int32(_HI16), F32)
    right = pltpu.bitcast(lax.shift_left(p, 16), F32)
    return jnp.concatenate([left, right], axis=-1).astype(BF16)


def _ln1_route_sort(pre, l1g_ref, l1b_ref, wrh_ref, wrl_ref, br_ref, x1_ref, xs_ref, rg_ref, cw_ref):
    for w in range(pre.shape[0] // TS_TAIL):
        rows = slice(w * TS_TAIL, (w + 1) * TS_TAIL)
        _ln1_route_sort_window(pre[rows, :], l1g_ref, l1b_ref, wrh_ref, wrl_ref, br_ref, x1_ref.at[rows, :],
                               xs_ref.at[w * WIN_ROWS:(w + 1) * WIN_ROWS, :], rg_ref.at[:, rows],
                               cw_ref.at[w * ROUTE_ROWS:(w + 1) * ROUTE_ROWS, :])


def _ln1_route_sort_window(pre, l1g_ref, l1b_ref, wrh_ref, wrl_ref, br_ref, x1_ref, xs_ref, rg_ref, cw_ref):
    ts = pre.shape[0]
    x1 = _ln(pre, l1g_ref[...], l1b_ref[...])
    x1_ref[...] = x1

    xh = x1.astype(BF16)
    xl = (x1 - xh.astype(F32)).astype(BF16)
    wrh = wrh_ref[...]
    logits = (lax.dot_general(wrh, xh, _NT, preferred_element_type=F32)
              + lax.dot_general(wrh, xl, _NT, preferred_element_type=F32)
              + lax.dot_general(wrl_ref[...], xh, _NT, preferred_element_type=F32)) + br_ref[...]

    row = lax.broadcasted_iota(I32, logits.shape, 0).astype(F32)
    neg = jnp.float32(-jnp.inf)
    no_row = jnp.float32(ROUTE_ROWS)
    is_grp = row < N_GROUPS
    gl = jnp.where(is_grp, logits, neg)
    gm = jnp.max(gl, axis=0, keepdims=True)
    gidx = jnp.min(jnp.where(gl == gm, row, no_row), axis=0, keepdims=True)
    denom = jnp.sum(jnp.where(is_grp, jnp.exp(gl - gm), 0.0), axis=0, keepdims=True)
    p_g = 1.0 / denom

    lo = ROUTE_ROW0 + EXPERTS_PER_GROUP * gidx
    sel = (row >= lo) & (row < lo + EXPERTS_PER_GROUP)
    sl = jnp.where(sel, logits, neg)
    m1 = jnp.max(sl, axis=0, keepdims=True)
    i1 = jnp.min(jnp.where(sl == m1, row, no_row), axis=0, keepdims=True)
    sl2 = jnp.where(row == i1, neg, sl)
    m2 = jnp.max(sl2, axis=0, keepdims=True)
    i2 = jnp.min(jnp.where(sl2 == m2, row, no_row), axis=0, keepdims=True)
    e21 = jnp.exp(m2 - m1)
    g0 = p_g / (1.0 + e21)
    g1 = p_g * e21 / (1.0 + e21)

    oh0 = row == i1
    oh1 = row == i2
    s_f = jnp.where(oh0 | oh1, 1.0, 0.0)
    rr = lax.broadcasted_iota(I32, (ts, ts), 0)
    cc = lax.broadcasted_iota(I32, (ts, ts), 1)
    earlier = jnp.where(rr < cc, 1.0, 0.0).astype(BF16)
    rank = jnp.dot(s_f.astype(BF16), earlier, preferred_element_type=F32)
    cnt = jnp.sum(s_f, axis=1, keepdims=True)
    run = jnp.ceil(cnt * (1.0 / SUBLANES)) * SUBLANES
    er = lax.broadcasted_iota(I32, (ROUTE_ROWS, ROUTE_ROWS), 0)
    ec = lax.broadcasted_iota(I32, (ROUTE_ROWS, ROUTE_ROWS), 1)
    lower = jnp.where(ec < er, 1.0, 0.0).astype(BF16)
    run_start = jnp.dot(lower, jnp.broadcast_to(run, (ROUTE_ROWS, LANES)).astype(BF16),
                        preferred_element_type=F32)[:, 0:1]
    where_to = rank + run_start
    lp0 = jnp.sum(jnp.where(oh0, where_to, 0.0), axis=0, keepdims=True)
    lp1 = jnp.sum(jnp.where(oh1, where_to, 0.0), axis=0, keepdims=True)

    srow = lax.broadcasted_iota(I32, (WIN_ROWS, ts), 0).astype(F32)
    place = jnp.where((srow == lp0) | (srow == lp1), 1.0, 0.0).astype(BF16)
    xs_ref[...] = _pack_pairs(jnp.dot(place, xh, preferred_element_type=F32))

    r8 = lax.broadcasted_iota(I32, (SUBLANES, ts), 0)
    rg_ref[...] = jnp.where(r8 == 0, g0, jnp.where(r8 == 1, g1, jnp.where(r8 == 2, lp0,
                            jnp.where(r8 == 3, lp1, 0.0))))
    cw_ref[...] = jnp.broadcast_to(cnt, cw_ref.shape)


def _tail_out_shapes(t, d):
    nw = t // TS_TAIL
    return (jax.ShapeDtypeStruct((t, d), F32),
            jax.ShapeDtypeStruct((nw * WIN_ROWS, d // 2), I32),
            jax.ShapeDtypeStruct((SUBLANES, t), F32),
            jax.ShapeDtypeStruct((nw * ROUTE_ROWS, SUBLANES), F32))


def _tail_out_specs(d, tok, tok_t):
    return (pl.BlockSpec((TS_STEP, d), tok),
            pl.BlockSpec((TAIL_WINDOWS * WIN_ROWS, d // 2), tok),
            pl.BlockSpec((SUBLANES, TS_STEP), tok_t),
            pl.BlockSpec((TAIL_WINDOWS * ROUTE_ROWS, SUBLANES), tok))


def _pw1_glu_kernel(x_ref, w_ref, b_ref, o_ref):
    d = o_ref.shape[-1]
    h = jnp.dot(x_ref[...].astype(BF16), w_ref[...], preferred_element_type=F32) + b_ref[...]
    o_ref[...] = h[:, :d] * jax.nn.sigmoid(h[:, d:])


def _pw1_glu(x, w_bf, b):
    t, d = x.shape
    tm = TM_PW1
    return pl.pallas_call(
        _pw1_glu_kernel,
        grid=(t // tm,),
        in_specs=[pl.BlockSpec((tm, d), lambda i: (i, 0)),
                  pl.BlockSpec((d, 2 * d), lambda i: (0, 0)),
                  pl.BlockSpec((1, 2 * d), lambda i: (0, 0))],
        out_specs=pl.BlockSpec((tm, d), lambda i: (i, 0)),
        out_shape=jax.ShapeDtypeStruct((t, d), F32),
        compiler_params=_cparams(("arbitrary",)),
        name="conv_pw1_glu",
    )(x, w_bf, b.reshape(1, -1))


def _conv_tail_kernel(hcur_ref, hprev_ref, x_ref, wdw_ref, bdw_ref, lng_ref, lnb_ref, wpw2_ref, bpw2_ref,
                      l1g_ref, l1b_ref, wrh_ref, wrl_ref, br_ref,
                      x1_ref, xs_ref, rg_ref, cw_ref,
                      hext_ref, hsh_ref, conv_ref):
    ts, d = hcur_ref.shape
    j = pl.program_id(1)

    hext_ref[HALO:, :] = hcur_ref[...]

    @pl.when(j == 0)
    def _():
        hext_ref[0:HALO, :] = jnp.zeros((HALO, d), F32)

    @pl.when(j > 0)
    def _():
        hext_ref[0:HALO, :] = hprev_ref[...]

    n_sh = ts + HALO - SUBLANES
    offs = [HALO - (CONV_WIDTH - 1) + k for k in range(CONV_WIDTH)]

    for c in range(d // LANES):
        lanes = slice(c * LANES, (c + 1) * LANES)
        for b in range(1, SUBLANES):
            hsh_ref[b - 1] = hext_ref[b:b + n_sh, lanes]
        w_rows = [jnp.broadcast_to(wdw_ref[k:k + 1, lanes], (SUBLANES, LANES)) for k in range(CONV_WIDTH)]
        bias = jnp.broadcast_to(bdw_ref[:, lanes], (SUBLANES, LANES))

        def conv_rows(r, carry, lanes=lanes, w_rows=w_rows, bias=bias):
            r0 = pl.multiple_of(r * CONV_ROWS, CONV_ROWS)
            for grp in range(CONV_ROWS // SUBLANES):
                acc = bias
                for k, off in enumerate(offs):
                    start = r0 + (grp + off // SUBLANES) * SUBLANES
                    if off % SUBLANES == 0:
                        tap = hext_ref[pl.ds(start, SUBLANES), lanes]
                    else:
                        tap = hsh_ref[off % SUBLANES - 1, pl.ds(start, SUBLANES), :]
                    acc = acc + w_rows[k] * tap
                conv_ref[pl.ds(r0 + grp * SUBLANES, SUBLANES), lanes] = acc
            return carry

        lax.fori_loop(0, ts // CONV_ROWS, conv_rows, 0)

    hn = _silu(_ln(conv_ref[...], lng_ref[...], lnb_ref[...]))
    mix = jnp.dot(hn.astype(BF16), wpw2_ref[...], preferred_element_type=F32) + bpw2_ref[...]
    pre = DEEPNORM_ALPHA * x_ref[...] + mix
    _ln1_route_sort(pre, l1g_ref, l1b_ref, wrh_ref, wrl_ref, br_ref, x1_ref, xs_ref, rg_ref, cw_ref)


def _conv_tail(h, x, batch, seq, w_dw, b_dw, ln_g, ln_b, w_pw2_bf, b_pw2, l1g, l1b, wrh, wrl, br):
    t, d = x.shape
    ts = TS_STEP
    nj = seq // ts
    halo_per_tile = ts // HALO
    h3 = h.reshape(batch, seq, d)
    x3 = x.reshape(batch, seq, d)
    row = lambda v: v.reshape(1, -1)
    const2 = lambda shape: pl.BlockSpec(shape, lambda b, j: (0, 0))
    tok = lambda b, j: (b * nj + j, 0)
    tok_t = lambda b, j: (0, b * nj + j)
    return pl.pallas_call(
        _conv_tail_kernel,
        grid=(batch, nj),
        in_specs=[pl.BlockSpec((None, ts, d), lambda b, j: (b, j, 0)),
                  pl.BlockSpec((None, HALO, d), lambda b, j: (b, jnp.maximum(j * halo_per_tile - 1, 0), 0)),
                  pl.BlockSpec((None, ts, d), lambda b, j: (b, j, 0)),
                  const2((CONV_WIDTH, d)), const2((1, d)), const2((1, d)), const2((1, d)),
                  const2((d, d)), const2((1, d)), const2((1, d)), const2((1, d)),
                  const2((ROUTE_ROWS, d)), const2((ROUTE_ROWS, d)), const2((ROUTE_ROWS, 1))],
        out_specs=_tail_out_specs(d, tok, tok_t),
        out_shape=_tail_out_shapes(t, d),
        scratch_shapes=[pltpu.VMEM((ts + HALO, d), F32), pltpu.VMEM((SUBLANES - 1, ts + HALO - SUBLANES, LANES), F32),
                        pltpu.VMEM((ts, d), F32)],
        compiler_params=_cparams(("arbitrary", "arbitrary")),
        name="conv_tail_ln1_route",
    )(h3, h3, x3, w_dw, row(b_dw), row(ln_g), row(ln_b), w_pw2_bf, row(b_pw2), row(l1g), row(l1b), wrh, wrl, br)


def _rope_table_kernel(pos_ref, invf_ref, cos_ref, sin_ref):
    ang = pos_ref[...].astype(F32) * invf_ref[...]
    cos_ref[...] = jnp.cos(ang)
    sin_ref[...] = jnp.sin(ang)


def _rope_tables(positions, half):
    t = positions.size
    tm = 1024
    inv_freq = ROPE_BASE ** (-jnp.arange(half, dtype=F32) / half)
    return pl.pallas_call(
        _rope_table_kernel,
        grid=(t // tm,),
        in_specs=[pl.BlockSpec((tm, 1), lambda i: (i, 0)), pl.BlockSpec((1, half), lambda i: (0, 0))],
        out_specs=(pl.BlockSpec((tm, half), lambda i: (i, 0)), pl.BlockSpec((tm, half), lambda i: (i, 0))),
        out_shape=(jax.ShapeDtypeStruct((t, half), F32), jax.ShapeDtypeStruct((t, half), F32)),
        compiler_params=_cparams(("arbitrary",)),
        name="rope_tables",
    )(positions.reshape(t, 1), inv_freq.reshape(1, half))


def _rotate(t, cos, sin, head_dim):
    half = head_dim // 2
    parts = []
    for h in range(t.shape[1] // head_dim):
        t1 = t[:, h * head_dim:h * head_dim + half]
        t2 = t[:, h * head_dim + half:(h + 1) * head_dim]
        parts.append(t1 * cos - t2 * sin)
        parts.append(t1 * sin + t2 * cos)
    return jnp.concatenate(parts, axis=-1)


def _qkvg_kernel(x_ref, wq_ref, wk_ref, wv_ref, wg_ref, cos_ref, sin_ref, q_ref, k_ref, v_ref, g_ref, *, head_qk):
    xb = x_ref[...].astype(BF16)
    cos = cos_ref[...]
    sin = sin_ref[...]
    q = jnp.dot(xb, wq_ref[...], preferred_element_type=F32)
    q_ref[...] = _rotate(q, cos, sin, head_qk).astype(BF16)
    k = jnp.dot(xb, wk_ref[...], preferred_element_type=F32)
    k_ref[...] = (_rotate(k, cos, sin, head_qk) * (head_qk ** -0.5)).astype(BF16)
    v_ref[...] = jnp.dot(xb, wv_ref[...], preferred_element_type=F32).astype(BF16)
    g_ref[...] = jnp.dot(xb, wg_ref[...], preferred_element_type=F32).astype(BF16)


def _qkvg(x, w_bf, qk, vd, cos, sin):
    t, d = x.shape
    head_qk = qk // RET_HEADS
    tm = TM_QKVG
    assert vd == 2 * qk
    tokb = lambda n: pl.BlockSpec((tm, n), lambda i: (i, 0))
    wcol = lambda n, j: pl.BlockSpec((d, n), lambda i: (0, j))
    return pl.pallas_call(
        functools.partial(_qkvg_kernel, head_qk=head_qk),
        grid=(t // tm,),
        in_specs=[tokb(d), wcol(qk, 0), wcol(qk, 1), wcol(vd, 1), wcol(vd, 2), tokb(head_qk // 2), tokb(head_qk // 2)],
        out_specs=(tokb(qk), tokb(qk), tokb(vd), tokb(vd)),
        out_shape=(jax.ShapeDtypeStruct((t, qk), BF16), jax.ShapeDtypeStruct((t, qk), BF16),
                   jax.ShapeDtypeStruct((t, vd), BF16), jax.ShapeDtypeStruct((t, vd), BF16)),
        compiler_params=_cparams(("arbitrary",)),
        name="ret_qkvg_rope",
    )(x, w_bf, w_bf, w_bf, w_bf, cos, sin)


def _ret_core_kernel(q_ref, k_ref, v_ref, g_ref, mask_ref, xi_ref, zeta_ref, dec_ref, gng_ref, gnb_ref,
                     y_ref, state_ref):
    blk = RET_BLOCK
    n_blk = q_ref.shape[0] // blk

    @pl.when(pl.program_id(2) == 0)
    def _():
        state_ref[...] = jnp.zeros_like(state_ref)

    mask = mask_ref[...]
    xi = xi_ref[...]
    zeta = zeta_ref[...]
    dec = dec_ref[0:1, 0:1]
    for n in range(n_blk):
        rows = slice(n * blk, (n + 1) * blk)
        q = q_ref[rows, :]
        k = k_ref[rows, :]
        v = v_ref[rows, :]
        state = state_ref[...]
        s = lax.dot_general(q, k, _NT, preferred_element_type=F32)
        p = (s * mask).astype(BF16)
        o = jnp.dot(p, v, preferred_element_type=F32)
        qx = (q.astype(F32) * xi).astype(BF16)
        o = o + jnp.dot(qx, state.astype(BF16), preferred_element_type=F32)
        kz = (k.astype(F32) * zeta).astype(BF16)
        state_ref[...] = state * dec + lax.dot_general(kz, v, _TN, preferred_element_type=F32)
        mu = jnp.mean(o, axis=-1, keepdims=True)
        oc = o - mu
        var = jnp.mean(oc * oc, axis=-1, keepdims=True)
        on = oc * lax.rsqrt(var + LN_EPS) * gng_ref[...] + gnb_ref[...]
        y_ref[rows, :] = (_silu(g_ref[rows, :].astype(F32)) * on).astype(BF16)


def _ret_tables():
    blk = RET_BLOCK
    log_gamma = jnp.log(1.0 - 2.0 ** (-5.0 - jnp.arange(RET_HEADS, dtype=F32)))
    idx = jnp.arange(blk, dtype=F32)
    dist = jnp.abs(idx[:, None] - idx[None, :])
    visible = (jnp.floor(idx[None, :] / RET_CHUNK) <= jnp.floor(idx[:, None] / RET_CHUNK))
    mask = jnp.where(visible[None], jnp.exp(log_gamma[:, None, None] * dist[None]), 0.0)
    xi = jnp.exp(log_gamma[:, None] * (idx[None, :] + 1.0))[..., None]
    zeta = jnp.exp(log_gamma[:, None] * (blk - 1.0 - idx[None, :]))[..., None]
    dec = jnp.broadcast_to(jnp.exp(log_gamma * blk)[:, None, None], (RET_HEADS, SUBLANES, LANES))
    return mask.astype(F32), xi.astype(F32), zeta.astype(F32), dec.astype(F32)


def _ret_core(q, k, v, g, gn_g, gn_b, batch, seq):
    t, qk = q.shape
    vd = v.shape[1]
    hq = qk // RET_HEADS
    hv = vd // RET_HEADS
    sb = RET_SUPER
    ns = seq // sb
    mask, xi, zeta, dec = _ret_tables()
    tokb = lambda n: pl.BlockSpec((sb, n), lambda b, h, s: (b * ns + s, h))
    headb = lambda r, c: pl.BlockSpec((None, r, c), lambda b, h, s: (h, 0, 0))
    return pl.pallas_call(
        _ret_core_kernel,
        grid=(batch, RET_HEADS, ns),
        in_specs=[tokb(hq), tokb(hq), tokb(hv), tokb(hv),
                  headb(RET_BLOCK, RET_BLOCK), headb(RET_BLOCK, 1), headb(RET_BLOCK, 1), headb(SUBLANES, LANES),
                  pl.BlockSpec((1, hv), lambda b, h, s: (0, h)), pl.BlockSpec((1, hv), lambda b, h, s: (0, h))],
        out_specs=tokb(hv),
        out_shape=jax.ShapeDtypeStruct((t, vd), BF16),
        scratch_shapes=[pltpu.VMEM((hq, hv), F32)],
        compiler_params=_cparams(("arbitrary", "arbitrary", "arbitrary")),
        name="ret_core",
    )(q, k, v, g, mask, xi, zeta, dec, gn_g.reshape(1, -1), gn_b.reshape(1, -1))


def _ret_tail_kernel(y_ref, x_ref, wo_ref, l1g_ref, l1b_ref, wrh_ref, wrl_ref, br_ref,
                     x1_ref, xs_ref, rg_ref, cw_ref):
    mix = jnp.dot(y_ref[...], wo_ref[...], preferred_element_type=F32)
    pre = DEEPNORM_ALPHA * x_ref[...] + mix
    _ln1_route_sort(pre, l1g_ref, l1b_ref, wrh_ref, wrl_ref, br_ref, x1_ref, xs_ref, rg_ref, cw_ref)


def _ret_tail(y, x, w_o_bf, l1g, l1b, wrh, wrl, br):
    t, d = x.shape
    vd = y.shape[1]
    ts = TS_STEP
    row = lambda v: v.reshape(1, -1)
    const = lambda shape: pl.BlockSpec(shape, lambda i: (0, 0))
    tok = lambda i: (i, 0)
    tok_t = lambda i: (0, i)
    return pl.pallas_call(
        _ret_tail_kernel,
        grid=(t // ts,),
        in_specs=[pl.BlockSpec((ts, vd), tok), pl.BlockSpec((ts, d), tok), const((vd, d)),
                  const((1, d)), const((1, d)), const((ROUTE_ROWS, d)), const((ROUTE_ROWS, d)), const((ROUTE_ROWS, 1))],
        out_specs=_tail_out_specs(d, tok, tok_t),
        out_shape=_tail_out_shapes(t, d),
        compiler_params=_cparams(("arbitrary",)),
        name="ret_tail_ln1_route",
    )(y, x, w_o_bf, row(l1g), row(l1b), wrh, wrl, br)


def _ffn_kernel(gsrc_ref, tile_e_ref, tile_nv_ref, nt_ref, xs_hbm, wg_ref, wu_ref, wd_ref, ys_hbm,
                xbuf_ref, obuf_ref, wgb_ref, wub_ref, wdb_ref, sem_in, sem_out):
    del xs_hbm
    i = pl.program_id(0)
    n_steps = pl.num_programs(0)
    nt = nt_ref[0]
    slot = i % 2

    def groups(tile, buf, sem, to_hbm, wait):
        count = tile_nv_ref[tile]

        def copy(in_hbm, in_buf):
            return pltpu.make_async_copy(in_buf, in_hbm, sem) if to_hbm else pltpu.make_async_copy(in_hbm, in_buf, sem)

        if wait:
            @pl.when(count > 0)
            def _():
                rows = pl.ds(0, count * SUBLANES)
                copy(ys_hbm.at[rows], buf.at[rows]).wait()
            return

        def one(j, priority):
            copy(_row_group(ys_hbm, gsrc_ref[tile * TILE_GROUPS + j]),
                 _row_group(buf, j * SUBLANES)).start(priority=priority)

        def pair(jj, carry):
            one(2 * jj, 0)
            one(2 * jj + 1, 1)
            return carry

        @pl.when(count == TILE_GROUPS)
        def _():
            lax.fori_loop(0, TILE_GROUPS // 2, pair, 0, unroll=8)

        @pl.when(count < TILE_GROUPS)
        def _():
            lax.fori_loop(0, lax.shift_right_logical(count, 1), pair, 0)

            @pl.when((count & 1) == 1)
            def _():
                one(count - 1, 0)

    @pl.when(i == 0)
    def _():
        xbuf_ref[...] = jnp.zeros(xbuf_ref.shape, xbuf_ref.dtype)
        groups(0, xbuf_ref.at[0], sem_in.at[0], to_hbm=False, wait=False)

    @pl.when((i >= 2) & (i - 2 < nt))
    def _():
        groups(i - 2, obuf_ref.at[slot], sem_out.at[slot], to_hbm=True, wait=True)

    @pl.when(i < nt)
    def _():
        @pl.when((i == 0) | (tile_e_ref[i] != tile_e_ref[jnp.maximum(i - 1, 0)]))
        def _():
            wgb_ref[...] = wg_ref[...].astype(BF16)
            wub_ref[...] = wu_ref[...].astype(BF16)
            wdb_ref[...] = wd_ref[...].astype(BF16)

        groups(i, xbuf_ref.at[slot], sem_in.at[slot], to_hbm=False, wait=True)

        @pl.when(i + 1 < nt)
        def _():
            groups(i + 1, xbuf_ref.at[1 - slot], sem_in.at[1 - slot], to_hbm=False, wait=False)

        x = _unpack_pairs(xbuf_ref[slot])
        a = jnp.dot(x, wgb_ref[...], preferred_element_type=F32)
        u = jnp.dot(x, wub_ref[...], preferred_element_type=F32)
        h = (_silu(a) * u).astype(BF16)
        obuf_ref[slot] = _pack_pairs(jnp.dot(h, wdb_ref[...], preferred_element_type=F32))
        groups(i, obuf_ref.at[slot], sem_out.at[slot], to_hbm=True, wait=False)

    @pl.when((i == n_steps - 1) & (i >= 1) & (i - 1 < nt))
    def _():
        groups(i - 1, obuf_ref.at[1 - slot], sem_out.at[1 - slot], to_hbm=True, wait=True)

    @pl.when((i == n_steps - 1) & (i < nt))
    def _():
        groups(i, obuf_ref.at[slot], sem_out.at[slot], to_hbm=True, wait=True)


def _ffn(xs, gsrc, tile_e, tile_nv, n_tiles, w_gate, w_up, w_down, layer, nt_max):
    d, f = w_gate.shape[-2:]
    assert xs.shape[1] * 2 == d
    tm = TM_FFN
    wspec = lambda r, c: pl.BlockSpec((None, None, r, c), lambda i, gs, te, nv, nt: (layer, te[i], 0, 0))
    grid_spec = pltpu.PrefetchScalarGridSpec(
        num_scalar_prefetch=4,
        grid=(nt_max,),
        in_specs=[pl.BlockSpec(memory_space=pl.ANY), wspec(d, f), wspec(d, f), wspec(f, d)],
        out_specs=pl.BlockSpec(memory_space=pl.ANY),
        scratch_shapes=[pltpu.VMEM((2, tm, d // 2), I32), pltpu.VMEM((2, tm, d // 2), I32),
                        pltpu.VMEM((d, f), BF16), pltpu.VMEM((d, f), BF16), pltpu.VMEM((f, d), BF16),
                        pltpu.SemaphoreType.DMA((2,)), pltpu.SemaphoreType.DMA((2,))],
    )
    return pl.pallas_call(
        _ffn_kernel,
        grid_spec=grid_spec,
        out_shape=jax.ShapeDtypeStruct(xs.shape, xs.dtype),
        input_output_aliases={4: 0},
        compiler_params=_cparams(("arbitrary",)),
        name="moe_expert_ffn",
    )(gsrc, tile_e, tile_nv, n_tiles, xs, w_gate, w_up, w_down)


def _combine_ln2_kernel(x1_ref, rg_ref, ys_ref, l2g_ref, l2b_ref, o_ref):
    info = rg_ref[...]
    g0, g1, lp0, lp1 = info[0:1, :], info[1:2, :], info[2:3, :], info[3:4, :]
    ts = info.shape[1]
    srow = lax.broadcasted_iota(I32, (WIN_ROWS, ts), 0).astype(F32)
    unsort = (jnp.where(srow == lp0, g0, 0.0) + jnp.where(srow == lp1, g1, 0.0)).astype(BF16)
    ffn = lax.dot_general(unsort, _unpack_pairs(ys_ref[...]), _TN, preferred_element_type=F32)
    o_ref[...] = _ln(DEEPNORM_ALPHA * x1_ref[...] + ffn, l2g_ref[...], l2b_ref[...])


def _combine_ln2(x1, rg, ys, l2g, l2b):
    t, d = x1.shape
    ts = TS_TAIL
    return pl.pallas_call(
        _combine_ln2_kernel,
        grid=(t // ts,),
        in_specs=[pl.BlockSpec((ts, d), lambda i: (i, 0)),
                  pl.BlockSpec((SUBLANES, ts), lambda i: (0, i)),
                  pl.BlockSpec((WIN_ROWS, d // 2), lambda i: (i, 0)),
                  pl.BlockSpec((1, d), lambda i: (0, 0)),
                  pl.BlockSpec((1, d), lambda i: (0, 0))],
        out_specs=pl.BlockSpec((ts, d), lambda i: (i, 0)),
        out_shape=jax.ShapeDtypeStruct((t, d), F32),
        compiler_params=_cparams(("arbitrary",)),
        name="moe_combine_ln2",
    )(x1, rg, ys, l2g.reshape(1, -1), l2b.reshape(1, -1))


def _pick_last(below, values, axis):
    first = lax.index_in_dim(values, 0, axis, keepdims=False)
    steps = lax.slice_in_dim(values, 1, None, axis=axis) - lax.slice_in_dim(values, 0, -1, axis=axis)
    return first + jnp.sum(jnp.where(below, steps, 0), axis=axis)


def _moe_tables(cw, nw, nt_max):
    tm = TM_FFN
    cnt = cw.reshape(nw, ROUTE_ROWS, SUBLANES)[:, ROUTE_ROW0:ROUTE_ROW0 + N_EXPERTS, 0].astype(I32)
    run = ((cnt + SUBLANES - 1) // SUBLANES) * SUBLANES
    loc = jnp.cumsum(run, axis=1) - run
    e_rows = jnp.sum(run, axis=0)
    e_pad = ((e_rows + tm - 1) // tm) * tm
    e_end = jnp.cumsum(e_pad)
    e_off = e_end - e_pad
    glob = e_off[None, :] + jnp.cumsum(run, axis=0) - run

    tile_start = jnp.arange(nt_max, dtype=I32) * tm
    done = e_end[None, :-1] <= tile_start[:, None]
    tile_e = jnp.sum(done.astype(I32), axis=1)
    real_end = _pick_last(done, (e_off + e_rows)[None, :], 1)
    tile_nv = jnp.clip(real_end - tile_start, 0, tm) // SUBLANES

    of_tile = lambda v: _pick_last(done[:, None, :], v[None, :, :], 2)
    win_row0 = jnp.arange(nw, dtype=I32)[:, None] * WIN_ROWS
    t_start, t_end, t_shift = of_tile(glob), of_tile(glob + run), of_tile(win_row0 + loc - glob)
    rows = tile_start[:, None] + jnp.arange(TILE_GROUPS, dtype=I32)[None, :] * SUBLANES
    below = t_start[:, None, 1:] <= rows[:, :, None]
    zero_group = WIN_ROWS - SUBLANES
    gsrc = jnp.where(rows < _pick_last(below, t_end[:, None, :], 2), rows + _pick_last(below, t_shift[:, None, :], 2),
                     zero_group).astype(I32).reshape(-1)
    n_tiles = (e_end[-1:] // tm).astype(I32)
    return gsrc, tile_e.astype(I32), tile_nv.astype(I32), n_tiles


def _moe(x1, xs, rg, cw, w_gate, w_up, w_down, layer, l2g, l2b):
    t = x1.shape[0]
    nw = t // TS_TAIL
    max_rows = TOP_K * t + nw * N_EXPERTS * (SUBLANES - 1) + N_EXPERTS * (TM_FFN - 1)
    nt_max = -(-max_rows // TM_FFN)
    gsrc, tile_e, tile_nv, n_tiles = _moe_tables(cw, nw, nt_max)
    ys = _ffn(xs, gsrc, tile_e, tile_nv, n_tiles, w_gate, w_up, w_down, layer, nt_max)
    return _combine_ln2(x1, rg, ys, l2g, l2b)


def _router_weights(w_grp, b_grp, w_route, b_route):
    d = w_grp.shape[0]
    used = N_GROUPS + N_EXPERTS
    w = jnp.concatenate([w_grp, w_route, jnp.zeros((d, ROUTE_ROWS - used), F32)], axis=1).T
    b = jnp.concatenate([b_grp, b_route, jnp.zeros((ROUTE_ROWS - used,), F32)]).reshape(ROUTE_ROWS, 1)
    wh = w.astype(BF16)
    wl = (w - wh.astype(F32)).astype(BF16)
    return wh, wl, b


def kernel(x, positions, conv_w_pw1, conv_b_pw1, conv_w_dw, conv_b_dw, conv_ln_g, conv_ln_b, conv_w_pw2, conv_b_pw2,
           ret_w_qkvg, ret_gn_g, ret_gn_b, ret_w_o, ln1_g, ln1_b, ln2_g, ln2_b, moe_w_grp, moe_b_grp, moe_w_route,
           moe_b_route, moe_w_gate, moe_w_up, moe_w_down):
    batch, seq, d = x.shape
    t = batch * seq
    qk = d
    vd = 2 * d
    xt = x.reshape(t, d)
    cos, sin = _rope_tables(positions, qk // RET_HEADS // 2)
    for i in range(DEPTH):
        j = i // N_MIXERS
        wrh, wrl, br = _router_weights(moe_w_grp[i], moe_b_grp[i], moe_w_route[i], moe_b_route[i])
        if i % N_MIXERS == 0:
            h = _pw1_glu(xt, conv_w_pw1[j].astype(BF16), conv_b_pw1[j])
            x1, xs, rg, cw = _conv_tail(h, xt, batch, seq, conv_w_dw[j], conv_b_dw[j], conv_ln_g[j], conv_ln_b[j],
                                        conv_w_pw2[j].astype(BF16), conv_b_pw2[j], ln1_g[i], ln1_b[i], wrh, wrl, br)
        else:
            q, k, v, g = _qkvg(xt, ret_w_qkvg[j].astype(BF16), qk, vd, cos, sin)
            y = _ret_core(q, k, v, g, ret_gn_g[j], ret_gn_b[j], batch, seq)
            x1, xs, rg, cw = _ret_tail(y, xt, ret_w_o[j].astype(BF16), ln1_g[i], ln1_b[i], wrh, wrl, br)
        xt = _moe(x1, xs, rg, cw, moe_w_gate, moe_w_up, moe_w_down, i, ln2_g[i], ln2_b[i])
    return xt.reshape(batch, seq, d)
```

```python
import functools

import jax
import jax.numpy as jnp
from jax import lax
from jax.experimental import pallas as pl
from jax.experimental.pallas import tpu as pltpu

F32 = jnp.float32
BF16 = jnp.bfloat16
I32 = jnp.int32

DEPTH = 4
N_MIXERS = 2
CONV_WIDTH = 31
RET_HEADS = 4
RET_CHUNK = 64
N_GROUPS = 4
EXPERTS_PER_GROUP = 8
N_EXPERTS = N_GROUPS * EXPERTS_PER_GROUP
TOP_K = 2
ROPE_BASE = 10000.0
DEEPNORM_ALPHA = (2.0 * DEPTH) ** 0.25
LN_EPS = 1e-5

LANES = 128
SUBLANES = 8
VMEM_LIMIT = 56 * 1024 * 1024

HALO = 32
CONV_ROWS = 64
TS_TAIL = 256
TAIL_WINDOWS = 2
TS_STEP = TAIL_WINDOWS * TS_TAIL
TM_PW1 = 512
TM_QKVG = 256
RET_BLOCK = 256
RET_SUPER = 1024
TM_FFN = 512
ROUTE_ROWS = LANES
ROUTE_ROW0 = N_GROUPS
WIN_ROWS = -(-(TOP_K * TS_TAIL + N_EXPERTS * (SUBLANES - 1) + SUBLANES) // LANES) * LANES
WIN_GROUPS = WIN_ROWS // SUBLANES
TILE_GROUPS = TM_FFN // SUBLANES

_NT = (((1,), (1,)), ((), ()))
_TN = (((0,), (0,)), ((), ()))


def _cparams(sem):
    return pltpu.CompilerParams(dimension_semantics=sem, vmem_limit_bytes=VMEM_LIMIT)


def _ln(x, g, b):
    mu = jnp.mean(x, axis=-1, keepdims=True)
    xc = x - mu
    var = jnp.mean(xc * xc, axis=-1, keepdims=True)
    return xc * lax.rsqrt(var + LN_EPS) * g + b


def _silu(x):
    return x * jax.nn.sigmoid(x)


def _row_group(ref, row):
    return ref.at[pl.ds(pl.multiple_of(row, SUBLANES), SUBLANES)]


_HI16 = -65536


def _pack_pairs(v):
    half = v.shape[1] // 2
    vb = v.astype(BF16).astype(F32)
    hi = pltpu.bitcast(vb[:, :half], I32) & jnp.int32(_HI16)
    lo = lax.shift_right_logical(pltpu.bitcast(vb[:, half:], I32), 16)
    return hi | lo


def _unpack_pairs(p):
    left = pltpu.bitcast(p & jnp.int32(_HI16), F32)
    right = pltpu.bitcast(lax.shift_left(p, 16), F32)
    return jnp.concatenate([left, right], axis=-1).astype(BF16)


def _ln1_route_sort(pre, l1g_ref, l1b_ref, wrh_ref, wrl_ref, br_ref, x1_ref, xs_ref, rg_ref, cw_ref):
    for w in range(pre.shape[0] // TS_TAIL):
        rows = slice(w * TS_TAIL, (w + 1) * TS_TAIL)
        _ln1_route_sort_window(pre[rows, :], l1g_ref, l1b_ref, wrh_ref, wrl_ref, br_ref, x1_ref.at[rows, :],
                               xs_ref.at[w * WIN_ROWS:(w + 1) * WIN_ROWS, :], rg_ref.at[:, rows],
                               cw_ref.at[w * ROUTE_ROWS:(w + 1) * ROUTE_ROWS, :])


def _ln1_route_sort_window(pre, l1g_ref, l1b_ref, wrh_ref, wrl_ref, br_ref, x1_ref, xs_ref, rg_ref, cw_ref):
    ts = pre.shape[0]
    x1 = _ln(pre, l1g_ref[...], l1b_ref[...])
    x1_ref[...] = x1

    xh = x1.astype(BF16)
    xl = (x1 - xh.astype(F32)).astype(BF16)
    wrh = wrh_ref[...]
    logits = (lax.dot_general(wrh, xh, _NT, preferred_element_type=F32)
              + lax.dot_general(wrh, xl, _NT, preferred_element_type=F32)
              + lax.dot_general(wrl_ref[...], xh, _NT, preferred_element_type=F32)) + br_ref[...]

    row = lax.broadcasted_iota(I32, logits.shape, 0).astype(F32)
    neg = jnp.float32(-jnp.inf)
    no_row = jnp.float32(ROUTE_ROWS)
    is_grp = row < N_GROUPS
    gl = jnp.where(is_grp, logits, neg)
    gm = jnp.max(gl, axis=0, keepdims=True)
    gidx = jnp.min(jnp.where(gl == gm, row, no_row), axis=0, keepdims=True)
    denom = jnp.sum(jnp.where(is_grp, jnp.exp(gl - gm), 0.0), axis=0, keepdims=True)
    p_g = 1.0 / denom

    lo = ROUTE_ROW0 + EXPERTS_PER_GROUP * gidx
    sel = (row >= lo) & (row < lo + EXPERTS_PER_GROUP)
    sl = jnp.where(sel, logits, neg)
    m1 = jnp.max(sl, axis=0, keepdims=True)
    i1 = jnp.min(jnp.where(sl == m1, row, no_row), axis=0, keepdims=True)
    sl2 = jnp.where(row == i1, neg, sl)
    m2 = jnp.max(sl2, axis=0, keepdims=True)
    i2 = jnp.min(jnp.where(sl2 == m2, row, no_row), axis=0, keepdims=True)
    e21 = jnp.exp(m2 - m1)
    g0 = p_g / (1.0 + e21)
    g1 = p_g * e21 / (1.0 + e21)

    oh0 = row == i1
    oh1 = row == i2
    s_f = jnp.where(oh0 | oh1, 1.0, 0.0)
    rr = lax.broadcasted_iota(I32, (ts, ts), 0)
    cc = lax.broadcasted_iota(I32, (ts, ts), 1)
    earlier = jnp.where(rr < cc, 1.0, 0.0).astype(BF16)
    rank = jnp.dot(s_f.astype(BF16), earlier, preferred_element_type=F32)
    cnt = jnp.sum(s_f, axis=1, keepdims=True)
    run = jnp.ceil(cnt * (1.0 / SUBLANES)) * SUBLANES
    er = lax.broadcasted_iota(I32, (ROUTE_ROWS, ROUTE_ROWS), 0)
    ec = lax.broadcasted_iota(I32, (ROUTE_ROWS, ROUTE_ROWS), 1)
    lower = jnp.where(ec < er, 1.0, 0.0).astype(BF16)
    run_start = jnp.dot(lower, jnp.broadcast_to(run, (ROUTE_ROWS, LANES)).astype(BF16),
                        preferred_element_type=F32)[:, 0:1]
    where_to = rank + run_start
    lp0 = jnp.sum(jnp.where(oh0, where_to, 0.0), axis=0, keepdims=True)
    lp1 = jnp.sum(jnp.where(oh1, where_to, 0.0), axis=0, keepdims=True)

    srow = lax.broadcasted_iota(I32, (WIN_ROWS, ts), 0).astype(F32)
    place = jnp.where((srow == lp0) | (srow == lp1), 1.0, 0.0).astype(BF16)
    xs_ref[...] = _pack_pairs(jnp.dot(place, xh, preferred_element_type=F32))

    r8 = lax.broadcasted_iota(I32, (SUBLANES, ts), 0)
    rg_ref[...] = jnp.where(r8 == 0, g0, jnp.where(r8 == 1, g1, jnp.where(r8 == 2, lp0,
                            jnp.where(r8 == 3, lp1, 0.0))))
    cw_ref[...] = jnp.broadcast_to(cnt, cw_ref.shape)


def _tail_out_shapes(t, d):
    nw = t // TS_TAIL
    return (jax.ShapeDtypeStruct((t, d), F32),
            jax.ShapeDtypeStruct((nw * WIN_ROWS, d // 2), I32),
            jax.ShapeDtypeStruct((SUBLANES, t), F32),
            jax.ShapeDtypeStruct((nw * ROUTE_ROWS, SUBLANES), F32))


def _tail_out_specs(d, tok, tok_t):
    return (pl.BlockSpec((TS_STEP, d), tok),
            pl.BlockSpec((TAIL_WINDOWS * WIN_ROWS, d // 2), tok),
            pl.BlockSpec((SUBLANES, TS_STEP), tok_t),
            pl.BlockSpec((TAIL_WINDOWS * ROUTE_ROWS, SUBLANES), tok))


def _pw1_glu_kernel(x_ref, w_ref, b_ref, o_ref):
    d = o_ref.shape[-1]
    h = jnp.dot(x_ref[...].astype(BF16), w_ref[...], preferred_element_type=F32) + b_ref[...]
    o_ref[...] = h[:, :d] * jax.nn.sigmoid(h[:, d:])


def _pw1_glu(x, w_bf, b):
    t, d = x.shape
    tm = TM_PW1
    return pl.pallas_call(
        _pw1_glu_kernel,
        grid=(t // tm,),
        in_specs=[pl.BlockSpec((tm, d), lambda i: (i, 0)),
                  pl.BlockSpec((d, 2 * d), lambda i: (0, 0)),
                  pl.BlockSpec((1, 2 * d), lambda i: (0, 0))],
        out_specs=pl.BlockSpec((tm, d), lambda i: (i, 0)),
        out_shape=jax.ShapeDtypeStruct((t, d), F32),
        compiler_params=_cparams(("arbitrary",)),
        name="conv_pw1_glu",
    )(x, w_bf, b.reshape(1, -1))


def _conv_tail_kernel(hcur_ref, hprev_ref, x_ref, wdw_ref, bdw_ref, lng_ref, lnb_ref, wpw2_ref, bpw2_ref,
                      l1g_ref, l1b_ref, wrh_ref, wrl_ref, br_ref,
                      x1_ref, xs_ref, rg_ref, cw_ref,
                      hext_ref, hsh_ref, conv_ref):
    ts, d = hcur_ref.shape
    j = pl.program_id(1)

    hext_ref[HALO:, :] = hcur_ref[...]

    @pl.when(j == 0)
    def _():
        hext_ref[0:HALO, :] = jnp.zeros((HALO, d), F32)

    @pl.when(j > 0)
    def _():
        hext_ref[0:HALO, :] = hprev_ref[...]

    n_sh = ts + HALO - SUBLANES
    offs = [HALO - (CONV_WIDTH - 1) + k for k in range(CONV_WIDTH)]

    for c in range(d // LANES):
        lanes = slice(c * LANES, (c + 1) * LANES)
        for b in range(1, SUBLANES):
            hsh_ref[b - 1] = hext_ref[b:b + n_sh, lanes]
        w_rows = [jnp.broadcast_to(wdw_ref[k:k + 1, lanes], (SUBLANES, LANES)) for k in range(CONV_WIDTH)]
        bias = jnp.broadcast_to(bdw_ref[:, lanes], (SUBLANES, LANES))

        def conv_rows(r, carry, lanes=lanes, w_rows=w_rows, bias=bias):
            r0 = pl.multiple_of(r * CONV_ROWS, CONV_ROWS)
            for grp in range(CONV_ROWS // SUBLANES):
                acc = bias
                for k, off in enumerate(offs):
                    start = r0 + (grp + off // SUBLANES) * SUBLANES
                    if off % SUBLANES == 0:
                        tap = hext_ref[pl.ds(start, SUBLANES), lanes]
                    else:
                        tap = hsh_ref[off % SUBLANES - 1, pl.ds(start, SUBLANES), :]
                    acc = acc + w_rows[k] * tap
                conv_ref[pl.ds(r0 + grp * SUBLANES, SUBLANES), lanes] = acc
            return carry

        lax.fori_loop(0, ts // CONV_ROWS, conv_rows, 0)

    hn = _silu(_ln(conv_ref[...], lng_ref[...], lnb_ref[...]))
    mix = jnp.dot(hn.astype(BF16), wpw2_ref[...], preferred_element_type=F32) + bpw2_ref[...]
    pre = DEEPNORM_ALPHA * x_ref[...] + mix
    _ln1_route_sort(pre, l1g_ref, l1b_ref, wrh_ref, wrl_ref, br_ref, x1_ref, xs_ref, rg_ref, cw_ref)


def _conv_tail(h, x, batch, seq, w_dw, b_dw, ln_g, ln_b, w_pw2_bf, b_pw2, l1g, l1b, wrh, wrl, br):
    t, d = x.shape
    ts = TS_STEP
    nj = seq // ts
    halo_per_tile = ts // HALO
    h3 = h.reshape(batch, seq, d)
    x3 = x.reshape(batch, seq, d)
    row = lambda v: v.reshape(1, -1)
    const2 = lambda shape: pl.BlockSpec(shape, lambda b, j: (0, 0))
    tok = lambda b, j: (b * nj + j, 0)
    tok_t = lambda b, j: (0, b * nj + j)
    return pl.pallas_call(
        _conv_tail_kernel,
        grid=(batch, nj),
        in_specs=[pl.BlockSpec((None, ts, d), lambda b, j: (b, j, 0)),
                  pl.BlockSpec((None, HALO, d), lambda b, j: (b, jnp.maximum(j * halo_per_tile - 1, 0), 0)),
                  pl.BlockSpec((None, ts, d), lambda b, j: (b, j, 0)),
                  const2((CONV_WIDTH, d)), const2((1, d)), const2((1, d)), const2((1, d)),
                  const2((d, d)), const2((1, d)), const2((1, d)), const2((1, d)),
                  const2((ROUTE_ROWS, d)), const2((ROUTE_ROWS, d)), const2((ROUTE_ROWS, 1))],
        out_specs=_tail_out_specs(d, tok, tok_t),
        out_shape=_tail_out_shapes(t, d),
        scratch_shapes=[pltpu.VMEM((ts + HALO, d), F32), pltpu.VMEM((SUBLANES - 1, ts + HALO - SUBLANES, LANES), F32),
                        pltpu.VMEM((ts, d), F32)],
        compiler_params=_cparams(("arbitrary", "arbitrary")),
        name="conv_tail_ln1_route",
    )(h3, h3, x3, w_dw, row(b_dw), row(ln_g), row(ln_b), w_pw2_bf, row(b_pw2), row(l1g), row(l1b), wrh, wrl, br)


def _rope_table_kernel(pos_ref, invf_ref, cos_ref, sin_ref):
    ang = pos_ref[...].astype(F32) * invf_ref[...]
    cos_ref[...] = jnp.cos(ang)
    sin_ref[...] = jnp.sin(ang)


def _rope_tables(positions, half):
    t = positions.size
    tm = 1024
    inv_freq = ROPE_BASE ** (-jnp.arange(half, dtype=F32) / half)
    return pl.pallas_call(
        _rope_table_kernel,
        grid=(t // tm,),
        in_specs=[pl.BlockSpec((tm, 1), lambda i: (i, 0)), pl.BlockSpec((1, half), lambda i: (0, 0))],
        out_specs=(pl.BlockSpec((tm, half), lambda i: (i, 0)), pl.BlockSpec((tm, half), lambda i: (i, 0))),
        out_shape=(jax.ShapeDtypeStruct((t, half), F32), jax.ShapeDtypeStruct((t, half), F32)),
        compiler_params=_cparams(("arbitrary",)),
        name="rope_tables",
    )(positions.reshape(t, 1), inv_freq.reshape(1, half))


def _rotate(t, cos, sin, head_dim):
    half = head_dim // 2
    parts = []
    for h in range(t.shape[1] // head_dim):
        t1 = t[:, h * head_dim:h * head_dim + half]
        t2 = t[:, h * head_dim + half:(h + 1) * head_dim]
        parts.append(t1 * cos - t2 * sin)
        parts.append(t1 * sin + t2 * cos)
    return jnp.concatenate(parts, axis=-1)


def _qkvg_kernel(x_ref, wq_ref, wk_ref, wv_ref, wg_ref, cos_ref, sin_ref, q_ref, k_ref, v_ref, g_ref, *, head_qk):
    xb = x_ref[...].astype(BF16)
    cos = cos_ref[...]
    sin = sin_ref[...]
    q = jnp.dot(xb, wq_ref[...], preferred_element_type=F32)
    q_ref[...] = _rotate(q, cos, sin, head_qk).astype(BF16)
    k = jnp.dot(xb, wk_ref[...], preferred_element_type=F32)
    k_ref[...] = (_rotate(k, cos, sin, head_qk) * (head_qk ** -0.5)).astype(BF16)
    v_ref[...] = jnp.dot(xb, wv_ref[...], preferred_element_type=F32).astype(BF16)
    g_ref[...] = jnp.dot(xb, wg_ref[...], preferred_element_type=F32).astype(BF16)


def _qkvg(x, w_bf, qk, vd, cos, sin):
    t, d = x.shape
    head_qk = qk // RET_HEADS
    tm = TM_QKVG
    assert vd == 2 * qk
    tokb = lambda n: pl.BlockSpec((tm, n), lambda i: (i, 0))
    wcol = lambda n, j: pl.BlockSpec((d, n), lambda i: (0, j))
    return pl.pallas_call(
        functools.partial(_qkvg_kernel, head_qk=head_qk),
        grid=(t // tm,),
        in_specs=[tokb(d), wcol(qk, 0), wcol(qk, 1), wcol(vd, 1), wcol(vd, 2), tokb(head_qk // 2), tokb(head_qk // 2)],
        out_specs=(tokb(qk), tokb(qk), tokb(vd), tokb(vd)),
        out_shape=(jax.ShapeDtypeStruct((t, qk), BF16), jax.ShapeDtypeStruct((t, qk), BF16),
                   jax.ShapeDtypeStruct((t, vd), BF16), jax.ShapeDtypeStruct((t, vd), BF16)),
        compiler_params=_cparams(("arbitrary",)),
        name="ret_qkvg_rope",
    )(x, w_bf, w_bf, w_bf, w_bf, cos, sin)


def _ret_core_kernel(q_ref, k_ref, v_ref, g_ref, mask_ref, xi_ref, zeta_ref, dec_ref, gng_ref, gnb_ref,
                     y_ref, state_ref):
    blk = RET_BLOCK
    n_blk = q_ref.shape[0] // blk

    @pl.when(pl.program_id(2) == 0)
    def _():
        state_ref[...] = jnp.zeros_like(state_ref)

    mask = mask_ref[...]
    xi = xi_ref[...]
    zeta = zeta_ref[...]
    dec = dec_ref[0:1, 0:1]
    for n in range(n_blk):
        rows = slice(n * blk, (n + 1) * blk)
        q = q_ref[rows, :]
        k = k_ref[rows, :]
        v = v_ref[rows, :]
        state = state_ref[...]
        s = lax.dot_general(q, k, _NT, preferred_element_type=F32)
        p = (s * mask).astype(BF16)
        o = jnp.dot(p, v, preferred_element_type=F32)
        qx = (q.astype(F32) * xi).astype(BF16)
        o = o + jnp.dot(qx, state.astype(BF16), preferred_element_type=F32)
        kz = (k.astype(F32) * zeta).astype(BF16)
        state_ref[...] = state * dec + lax.dot_general(kz, v, _TN, preferred_element_type=F32)
        mu = jnp.mean(o, axis=-1, keepdims=True)
        oc = o - mu
        var = jnp.mean(oc * oc, axis=-1, keepdims=True)
        on = oc * lax.rsqrt(var + LN_EPS) * gng_ref[...] + gnb_ref[...]
        y_ref[rows, :] = (_silu(g_ref[rows, :].astype(F32)) * on).astype(BF16)


def _ret_tables():
    blk = RET_BLOCK
    log_gamma = jnp.log(1.0 - 2.0 ** (-5.0 - jnp.arange(RET_HEADS, dtype=F32)))
    idx = jnp.arange(blk, dtype=F32)
    dist = jnp.abs(idx[:, None] - idx[None, :])
    visible = (jnp.floor(idx[None, :] / RET_CHUNK) <= jnp.floor(idx[:, None] / RET_CHUNK))
    mask = jnp.where(visible[None], jnp.exp(log_gamma[:, None, None] * dist[None]), 0.0)
    xi = jnp.exp(log_gamma[:, None] * (idx[None, :] + 1.0))[..., None]
    zeta = jnp.exp(log_gamma[:, None] * (blk - 1.0 - idx[None, :]))[..., None]
    dec = jnp.broadcast_to(jnp.exp(log_gamma * blk)[:, None, None], (RET_HEADS, SUBLANES, LANES))
    return mask.astype(F32), xi.astype(F32), zeta.astype(F32), dec.astype(F32)


def _ret_core(q, k, v, g, gn_g, gn_b, batch, seq):
    t, qk = q.shape
    vd = v.shape[1]
    hq = qk // RET_HEADS
    hv = vd // RET_HEADS
    sb = RET_SUPER
    ns = seq // sb
    mask, xi, zeta, dec = _ret_tables()
    tokb = lambda n: pl.BlockSpec((sb, n), lambda b, h, s: (b * ns + s, h))
    headb = lambda r, c: pl.BlockSpec((None, r, c), lambda b, h, s: (h, 0, 0))
    return pl.pallas_call(
        _ret_core_kernel,
        grid=(batch, RET_HEADS, ns),
        in_specs=[tokb(hq), tokb(hq), tokb(hv), tokb(hv),
                  headb(RET_BLOCK, RET_BLOCK), headb(RET_BLOCK, 1), headb(RET_BLOCK, 1), headb(SUBLANES, LANES),
                  pl.BlockSpec((1, hv), lambda b, h, s: (0, h)), pl.BlockSpec((1, hv), lambda b, h, s: (0, h))],
        out_specs=tokb(hv),
        out_shape=jax.ShapeDtypeStruct((t, vd), BF16),
        scratch_shapes=[pltpu.VMEM((hq, hv), F32)],
        compiler_params=_cparams(("arbitrary", "arbitrary", "arbitrary")),
        name="ret_core",
    )(q, k, v, g, mask, xi, zeta, dec, gn_g.reshape(1, -1), gn_b.reshape(1, -1))


def _ret_tail_kernel(y_ref, x_ref, wo_ref, l1g_ref, l1b_ref, wrh_ref, wrl_ref, br_ref,
                     x1_ref, xs_ref, rg_ref, cw_ref):
    mix = jnp.dot(y_ref[...], wo_ref[...], preferred_element_type=F32)
    pre = DEEPNORM_ALPHA * x_ref[...] + mix
    _ln1_route_sort(pre, l1g_ref, l1b_ref, wrh_ref, wrl_ref, br_ref, x1_ref, xs_ref, rg_ref, cw_ref)


def _ret_tail(y, x, w_o_bf, l1g, l1b, wrh, wrl, br):
    t, d = x.shape
    vd = y.shape[1]
    ts = TS_STEP
    row = lambda v: v.reshape(1, -1)
    const = lambda shape: pl.BlockSpec(shape, lambda i: (0, 0))
    tok = lambda i: (i, 0)
    tok_t = lambda i: (0, i)
    return pl.pallas_call(
        _ret_tail_kernel,
        grid=(t // ts,),
        in_specs=[pl.BlockSpec((ts, vd), tok), pl.BlockSpec((ts, d), tok), const((vd, d)),
                  const((1, d)), const((1, d)), const((ROUTE_ROWS, d)), const((ROUTE_ROWS, d)), const((ROUTE_ROWS, 1))],
        out_specs=_tail_out_specs(d, tok, tok_t),
        out_shape=_tail_out_shapes(t, d),
        compiler_params=_cparams(("arbitrary",)),
        name="ret_tail_ln1_route",
    )(y, x, w_o_bf, row(l1g), row(l1b), wrh, wrl, br)


def _ffn_kernel(gsrc_ref, tile_e_ref, tile_nv_ref, nt_ref, xs_hbm, wg_ref, wu_ref, wd_ref, ys_hbm,
                xbuf_ref, obuf_ref, wgb_ref, wub_ref, wdb_ref, sem_in, sem_out):
    del xs_hbm
    i = pl.program_id(0)
    n_steps = pl.num_programs(0)
    nt = nt_ref[0]
    slot = i % 2

    def groups(tile, buf, sem, to_hbm, wait):
        count = tile_nv_ref[tile]

        def copy(in_hbm, in_buf):
            return pltpu.make_async_copy(in_buf, in_hbm, sem) if to_hbm else pltpu.make_async_copy(in_hbm, in_buf, sem)

        if wait:
            @pl.when(count > 0)
            def _():
                rows = pl.ds(0, count * SUBLANES)
                copy(ys_hbm.at[rows], buf.at[rows]).wait()
            return

        def one(j, priority):
            copy(_row_group(ys_hbm, gsrc_ref[tile * TILE_GROUPS + j]),
                 _row_group(buf, j * SUBLANES)).start(priority=priority)

        def pair(jj, carry):
            one(2 * jj, 0)
            one(2 * jj + 1, 1)
            return carry

        @pl.when(count == TILE_GROUPS)
        def _():
            lax.fori_loop(0, TILE_GROUPS // 2, pair, 0, unroll=8)

        @pl.when(count < TILE_GROUPS)
        def _():
            lax.fori_loop(0, lax.shift_right_logical(count, 1), pair, 0)

            @pl.when((count & 1) == 1)
            def _():
                one(count - 1, 0)

    @pl.when(i == 0)
    def _():
        xbuf_ref[...] = jnp.zeros(xbuf_ref.shape, xbuf_ref.dtype)
        groups(0, xbuf_ref.at[0], sem_in.at[0], to_hbm=False, wait=False)

    @pl.when((i >= 2) & (i - 2 < nt))
    def _():
        groups(i - 2, obuf_ref.at[slot], sem_out.at[slot], to_hbm=True, wait=True)

    @pl.when(i < nt)
    def _():
        @pl.when((i == 0) | (tile_e_ref[i] != tile_e_ref[jnp.maximum(i - 1, 0)]))
        def _():
            wgb_ref[...] = wg_ref[...].astype(BF16)
            wub_ref[...] = wu_ref[...].astype(BF16)
            wdb_ref[...] = wd_ref[...].astype(BF16)

        groups(i, xbuf_ref.at[slot], sem_in.at[slot], to_hbm=False, wait=True)

        def compute_tile():
            x = _unpack_pairs(xbuf_ref[slot])
            a = jnp.dot(x, wgb_ref[...], preferred_element_type=F32)
            u = jnp.dot(x, wub_ref[...], preferred_element_type=F32)
            h = (_silu(a) * u).astype(BF16)
            obuf_ref[slot] = _pack_pairs(jnp.dot(h, wdb_ref[...], preferred_element_type=F32))

        def start_all(tile, buf, sem, to_hbm):
            for j in range(TILE_GROUPS):
                in_hbm = _row_group(ys_hbm, gsrc_ref[tile * TILE_GROUPS + j])
                in_buf = _row_group(buf, j * SUBLANES)
                cp = pltpu.make_async_copy(in_buf, in_hbm, sem) if to_hbm else pltpu.make_async_copy(in_hbm, in_buf, sem)
                cp.start(priority=j % 2)

        nxt = jnp.minimum(i + 1, n_steps - 1)
        full = (tile_nv_ref[i] == TILE_GROUPS) & (i + 1 < nt) & (tile_nv_ref[nxt] == TILE_GROUPS)

        @pl.when(full)
        def _():
            start_all(i + 1, xbuf_ref.at[1 - slot], sem_in.at[1 - slot], to_hbm=False)
            compute_tile()
            start_all(i, obuf_ref.at[slot], sem_out.at[slot], to_hbm=True)

        @pl.when(jnp.logical_not(full))
        def _():
            @pl.when(i + 1 < nt)
            def _():
                groups(i + 1, xbuf_ref.at[1 - slot], sem_in.at[1 - slot], to_hbm=False, wait=False)

            compute_tile()
            groups(i, obuf_ref.at[slot], sem_out.at[slot], to_hbm=True, wait=False)

    @pl.when((i == n_steps - 1) & (i >= 1) & (i - 1 < nt))
    def _():
        groups(i - 1, obuf_ref.at[1 - slot], sem_out.at[1 - slot], to_hbm=True, wait=True)

    @pl.when((i == n_steps - 1) & (i < nt))
    def _():
        groups(i, obuf_ref.at[slot], sem_out.at[slot], to_hbm=True, wait=True)


def _ffn(xs, gsrc, tile_e, tile_nv, n_tiles, w_gate, w_up, w_down, layer, nt_max):
    d, f = w_gate.shape[-2:]
    assert xs.shape[1] * 2 == d
    tm = TM_FFN
    wspec = lambda r, c: pl.BlockSpec((None, None, r, c), lambda i, gs, te, nv, nt: (layer, te[i], 0, 0))
    grid_spec = pltpu.PrefetchScalarGridSpec(
        num_scalar_prefetch=4,
        grid=(nt_max,),
        in_specs=[pl.BlockSpec(memory_space=pl.ANY), wspec(d, f), wspec(d, f), wspec(f, d)],
        out_specs=pl.BlockSpec(memory_space=pl.ANY),
        scratch_shapes=[pltpu.VMEM((2, tm, d // 2), I32), pltpu.VMEM((2, tm, d // 2), I32),
                        pltpu.VMEM((d, f), BF16), pltpu.VMEM((d, f), BF16), pltpu.VMEM((f, d), BF16),
                        pltpu.SemaphoreType.DMA((2,)), pltpu.SemaphoreType.DMA((2,))],
    )
    return pl.pallas_call(
        _ffn_kernel,
        grid_spec=grid_spec,
        out_shape=jax.ShapeDtypeStruct(xs.shape, xs.dtype),
        input_output_aliases={4: 0},
        compiler_params=_cparams(("arbitrary",)),
        name="moe_expert_ffn",
    )(gsrc, tile_e, tile_nv, n_tiles, xs, w_gate, w_up, w_down)


def _combine_ln2_kernel(x1_ref, rg_ref, ys_ref, l2g_ref, l2b_ref, o_ref):
    ts = TS_TAIL
    srow = lax.broadcasted_iota(I32, (WIN_ROWS, ts), 0).astype(F32)
    for w in range(o_ref.shape[0] // ts):
        rows = slice(w * ts, (w + 1) * ts)
        info = rg_ref[:, rows]
        g0, g1, lp0, lp1 = info[0:1, :], info[1:2, :], info[2:3, :], info[3:4, :]
        unsort = (jnp.where(srow == lp0, g0, 0.0) + jnp.where(srow == lp1, g1, 0.0)).astype(BF16)
        ys = _unpack_pairs(ys_ref[w * WIN_ROWS:(w + 1) * WIN_ROWS, :])
        ffn = lax.dot_general(unsort, ys, _TN, preferred_element_type=F32)
        o_ref[rows, :] = _ln(DEEPNORM_ALPHA * x1_ref[rows, :] + ffn, l2g_ref[...], l2b_ref[...])


def _combine_ln2(x1, rg, ys, l2g, l2b):
    t, d = x1.shape
    ts = TS_STEP
    return pl.pallas_call(
        _combine_ln2_kernel,
        grid=(t // ts,),
        in_specs=[pl.BlockSpec((ts, d), lambda i: (i, 0)),
                  pl.BlockSpec((SUBLANES, ts), lambda i: (0, i)),
                  pl.BlockSpec((TAIL_WINDOWS * WIN_ROWS, d // 2), lambda i: (i, 0)),
                  pl.BlockSpec((1, d), lambda i: (0, 0)),
                  pl.BlockSpec((1, d), lambda i: (0, 0))],
        out_specs=pl.BlockSpec((ts, d), lambda i: (i, 0)),
        out_shape=jax.ShapeDtypeStruct((t, d), F32),
        compiler_params=_cparams(("arbitrary",)),
        name="moe_combine_ln2",
    )(x1, rg, ys, l2g.reshape(1, -1), l2b.reshape(1, -1))


def _pick_last(below, values, axis):
    first = lax.index_in_dim(values, 0, axis, keepdims=False)
    steps = lax.slice_in_dim(values, 1, None, axis=axis) - lax.slice_in_dim(values, 0, -1, axis=axis)
    return first + jnp.sum(jnp.where(below, steps, 0), axis=axis)


def _moe_tables(cw, nw, nt_max):
    tm = TM_FFN
    cnt = cw.reshape(nw, ROUTE_ROWS, SUBLANES)[:, ROUTE_ROW0:ROUTE_ROW0 + N_EXPERTS, 0].astype(I32)
    run = ((cnt + SUBLANES - 1) // SUBLANES) * SUBLANES
    loc = jnp.cumsum(run, axis=1) - run
    e_rows = jnp.sum(run, axis=0)
    e_pad = ((e_rows + tm - 1) // tm) * tm
    e_end = jnp.cumsum(e_pad)
    e_off = e_end - e_pad
    glob = e_off[None, :] + jnp.cumsum(run, axis=0) - run

    tile_start = jnp.arange(nt_max, dtype=I32) * tm
    done = e_end[None, :-1] <= tile_start[:, None]
    tile_e = jnp.sum(done.astype(I32), axis=1)
    real_end = _pick_last(done, (e_off + e_rows)[None, :], 1)
    tile_nv = jnp.clip(real_end - tile_start, 0, tm) // SUBLANES

    of_tile = lambda v: _pick_last(done[:, None, :], v[None, :, :], 2)
    win_row0 = jnp.arange(nw, dtype=I32)[:, None] * WIN_ROWS
    t_start, t_end, t_shift = of_tile(glob), of_tile(glob + run), of_tile(win_row0 + loc - glob)
    rows = tile_start[:, None] + jnp.arange(TILE_GROUPS, dtype=I32)[None, :] * SUBLANES
    below = t_start[:, None, 1:] <= rows[:, :, None]
    zero_group = WIN_ROWS - SUBLANES
    gsrc = jnp.where(rows < _pick_last(below, t_end[:, None, :], 2), rows + _pick_last(below, t_shift[:, None, :], 2),
                     zero_group).astype(I32).reshape(-1)
    n_tiles = (e_end[-1:] // tm).astype(I32)
    return gsrc, tile_e.astype(I32), tile_nv.astype(I32), n_tiles


def _moe(x1, xs, rg, cw, w_gate, w_up, w_down, layer, l2g, l2b):
    t = x1.shape[0]
    nw = t // TS_TAIL
    max_rows = TOP_K * t + nw * N_EXPERTS * (SUBLANES - 1) + N_EXPERTS * (TM_FFN - 1)
    nt_max = -(-max_rows // TM_FFN)
    gsrc, tile_e, tile_nv, n_tiles = _moe_tables(cw, nw, nt_max)
    ys = _ffn(xs, gsrc, tile_e, tile_nv, n_tiles, w_gate, w_up, w_down, layer, nt_max)
    return _combine_ln2(x1, rg, ys, l2g, l2b)


def _router_weights(w_grp, b_grp, w_route, b_route):
    d = w_grp.shape[0]
    used = N_GROUPS + N_EXPERTS
    w = jnp.concatenate([w_grp, w_route, jnp.zeros((d, ROUTE_ROWS - used), F32)], axis=1).T
    b = jnp.concatenate([b_grp, b_route, jnp.zeros((ROUTE_ROWS - used,), F32)]).reshape(ROUTE_ROWS, 1)
    wh = w.astype(BF16)
    wl = (w - wh.astype(F32)).astype(BF16)
    return wh, wl, b


def kernel(x, positions, conv_w_pw1, conv_b_pw1, conv_w_dw, conv_b_dw, conv_ln_g, conv_ln_b, conv_w_pw2, conv_b_pw2,
           ret_w_qkvg, ret_gn_g, ret_gn_b, ret_w_o, ln1_g, ln1_b, ln2_g, ln2_b, moe_w_grp, moe_b_grp, moe_w_route,
           moe_b_route, moe_w_gate, moe_w_up, moe_w_down):
    batch, seq, d = x.shape
    t = batch * seq
    qk = d
    vd = 2 * d
    xt = x.reshape(t, d)
    cos, sin = _rope_tables(positions, qk // RET_HEADS // 2)
    for i in range(DEPTH):
        j = i // N_MIXERS
        wrh, wrl, br = _router_weights(moe_w_grp[i], moe_b_grp[i], moe_w_route[i], moe_b_route[i])
        if i % N_MIXERS == 0:
            h = _pw1_glu(xt, conv_w_pw1[j].astype(BF16), conv_b_pw1[j])
            x1, xs, rg, cw = _conv_tail(h, xt, batch, seq, conv_w_dw[j], conv_b_dw[j], conv_ln_g[j], conv_ln_b[j],
                                        conv_w_pw2[j].astype(BF16), conv_b_pw2[j], ln1_g[i], ln1_b[i], wrh, wrl, br)
        else:
            q, k, v, g = _qkvg(xt, ret_w_qkvg[j].astype(BF16), qk, vd, cos, sin)
            y = _ret_core(q, k, v, g, ret_gn_g[j], ret_gn_b[j], batch, seq)
            x1, xs, rg, cw = _ret_tail(y, xt, ret_w_o[j].astype(BF16), ln1_g[i], ln1_b[i], wrh, wrl, br)
        xt = _moe(x1, xs, rg, cw, moe_w_gate, moe_w_up, moe_w_down, i, ln2_g[i], ln2_b[i])
    return xt.reshape(batch, seq, d)
```

```python
import functools

import jax
import jax.numpy as jnp
from jax import lax
from jax.experimental import pallas as pl
from jax.experimental.pallas import tpu as pltpu

F32 = jnp.float32
BF16 = jnp.bfloat16
I32 = jnp.int32

DEPTH = 4
N_MIXERS = 2
CONV_WIDTH = 31
RET_HEADS = 4
RET_CHUNK = 64
N_GROUPS = 4
EXPERTS_PER_GROUP = 8
N_EXPERTS = N_GROUPS * EXPERTS_PER_GROUP
TOP_K = 2
ROPE_BASE = 10000.0
DEEPNORM_ALPHA = (2.0 * DEPTH) ** 0.25
LN_EPS = 1e-5

LANES = 128
SUBLANES = 8
VMEM_LIMIT = 56 * 1024 * 1024

HALO = 32
CONV_ROWS = 64
TS_TAIL = 256
TAIL_WINDOWS = 2
TS_STEP = TAIL_WINDOWS * TS_TAIL
TM_PW1 = 512
TM_QKVG = 256
RET_BLOCK = 256
RET_SUPER = 1024
TM_FFN = 512
ROUTE_ROWS = LANES
ROUTE_ROW0 = N_GROUPS
WIN_ROWS = -(-(TOP_K * TS_TAIL + N_EXPERTS * (SUBLANES - 1) + SUBLANES) // LANES) * LANES
WIN_GROUPS = WIN_ROWS // SUBLANES
TILE_GROUPS = TM_FFN // SUBLANES

_NT = (((1,), (1,)), ((), ()))
_TN = (((0,), (0,)), ((), ()))


def _cparams(sem):
    return pltpu.CompilerParams(dimension_semantics=sem, vmem_limit_bytes=VMEM_LIMIT)


def _ln(x, g, b):
    mu = jnp.mean(x, axis=-1, keepdims=True)
    xc = x - mu
    var = jnp.mean(xc * xc, axis=-1, keepdims=True)
    return xc * lax.rsqrt(var + LN_EPS) * g + b


def _silu(x):
    return x * jax.nn.sigmoid(x)


def _row_group(ref, row):
    return ref.at[pl.ds(pl.multiple_of(row, SUBLANES), SUBLANES)]


_HI16 = -65536


def _pack_pairs(v):
    half = v.shape[1] // 2
    vb = v.astype(BF16).astype(F32)
    hi = pltpu.bitcast(vb[:, :half], I32) & jnp.int32(_HI16)
    lo = lax.shift_right_logical(pltpu.bitcast(vb[:, half:], I32), 16)
    return hi | lo


def _unpack_pairs(p):
    left = pltpu.bitcast(p & jnp.int32(_HI16), F32)
    right = pltpu.bitcast(lax.shift_left(p, 16), F32)
    return jnp.concatenate([left, right], axis=-1).astype(BF16)


def _ln1_route_sort(pre, l1g_ref, l1b_ref, wrh_ref, wrl_ref, br_ref, x1_ref, xs_ref, rg_ref, cw_ref):
    for w in range(pre.shape[0] // TS_TAIL):
        rows = slice(w * TS_TAIL, (w + 1) * TS_TAIL)
        _ln1_route_sort_window(pre[rows, :], l1g_ref, l1b_ref, wrh_ref, wrl_ref, br_ref, x1_ref.at[rows, :],
                               xs_ref.at[w * WIN_ROWS:(w + 1) * WIN_ROWS, :], rg_ref.at[:, rows],
                               cw_ref.at[w * ROUTE_ROWS:(w + 1) * ROUTE_ROWS, :])


def _ln1_route_sort_window(pre, l1g_ref, l1b_ref, wrh_ref, wrl_ref, br_ref, x1_ref, xs_ref, rg_ref, cw_ref):
    ts = pre.shape[0]
    x1 = _ln(pre, l1g_ref[...], l1b_ref[...])
    x1_ref[...] = x1

    xh = x1.astype(BF16)
    xl = (x1 - xh.astype(F32)).astype(BF16)
    wrh = wrh_ref[...]
    logits = (lax.dot_general(wrh, xh, _NT, preferred_element_type=F32)
              + lax.dot_general(wrh, xl, _NT, preferred_element_type=F32)
              + lax.dot_general(wrl_ref[...], xh, _NT, preferred_element_type=F32)) + br_ref[...]

    row = lax.broadcasted_iota(I32, logits.shape, 0).astype(F32)
    neg = jnp.float32(-jnp.inf)
    no_row = jnp.float32(ROUTE_ROWS)
    is_grp = row < N_GROUPS
    gl = jnp.where(is_grp, logits, neg)
    gm = jnp.max(gl, axis=0, keepdims=True)
    gidx = jnp.min(jnp.where(gl == gm, row, no_row), axis=0, keepdims=True)
    denom = jnp.sum(jnp.where(is_grp, jnp.exp(gl - gm), 0.0), axis=0, keepdims=True)
    p_g = 1.0 / denom

    lo = ROUTE_ROW0 + EXPERTS_PER_GROUP * gidx
    sel = (row >= lo) & (row < lo + EXPERTS_PER_GROUP)
    sl = jnp.where(sel, logits, neg)
    m1 = jnp.max(sl, axis=0, keepdims=True)
    i1 = jnp.min(jnp.where(sl == m1, row, no_row), axis=0, keepdims=True)
    sl2 = jnp.where(row == i1, neg, sl)
    m2 = jnp.max(sl2, axis=0, keepdims=True)
    i2 = jnp.min(jnp.where(sl2 == m2, row, no_row), axis=0, keepdims=True)
    e21 = jnp.exp(m2 - m1)
    g0 = p_g / (1.0 + e21)
    g1 = p_g * e21 / (1.0 + e21)

    oh0 = row == i1
    oh1 = row == i2
    s_f = jnp.where(oh0 | oh1, 1.0, 0.0)
    rr = lax.broadcasted_iota(I32, (ts, ts), 0)
    cc = lax.broadcasted_iota(I32, (ts, ts), 1)
    earlier = jnp.where(rr < cc, 1.0, 0.0).astype(BF16)
    rank = jnp.dot(s_f.astype(BF16), earlier, preferred_element_type=F32)
    cnt = jnp.sum(s_f, axis=1, keepdims=True)
    run = jnp.ceil(cnt * (1.0 / SUBLANES)) * SUBLANES
    er = lax.broadcasted_iota(I32, (ROUTE_ROWS, ROUTE_ROWS), 0)
    ec = lax.broadcasted_iota(I32, (ROUTE_ROWS, ROUTE_ROWS), 1)
    lower = jnp.where(ec < er, 1.0, 0.0).astype(BF16)
    run_start = jnp.dot(lower, jnp.broadcast_to(run, (ROUTE_ROWS, LANES)).astype(BF16),
                        preferred_element_type=F32)[:, 0:1]
    where_to = rank + run_start
    lp0 = jnp.sum(jnp.where(oh0, where_to, 0.0), axis=0, keepdims=True)
    lp1 = jnp.sum(jnp.where(oh1, where_to, 0.0), axis=0, keepdims=True)

    srow = lax.broadcasted_iota(I32, (WIN_ROWS, ts), 0).astype(F32)
    place = jnp.where((srow == lp0) | (srow == lp1), 1.0, 0.0).astype(BF16)
    xs_ref[...] = _pack_pairs(jnp.dot(place, xh, preferred_element_type=F32))

    r8 = lax.broadcasted_iota(I32, (SUBLANES, ts), 0)
    rg_ref[...] = jnp.where(r8 == 0, g0, jnp.where(r8 == 1, g1, jnp.where(r8 == 2, lp0,
                            jnp.where(r8 == 3, lp1, 0.0))))
    cw_ref[...] = jnp.broadcast_to(cnt, cw_ref.shape)


def _tail_out_shapes(t, d):
    nw = t // TS_TAIL
    return (jax.ShapeDtypeStruct((t, d), F32),
            jax.ShapeDtypeStruct((nw * WIN_ROWS, d // 2), I32),
            jax.ShapeDtypeStruct((SUBLANES, t), F32),
            jax.ShapeDtypeStruct((nw * ROUTE_ROWS, SUBLANES), F32))


def _tail_out_specs(d, tok, tok_t):
    return (pl.BlockSpec((TS_STEP, d), tok),
            pl.BlockSpec((TAIL_WINDOWS * WIN_ROWS, d // 2), tok),
            pl.BlockSpec((SUBLANES, TS_STEP), tok_t),
            pl.BlockSpec((TAIL_WINDOWS * ROUTE_ROWS, SUBLANES), tok))


def _pw1_glu_kernel(x_ref, w_ref, b_ref, o_ref):
    d = o_ref.shape[-1]
    h = jnp.dot(x_ref[...].astype(BF16), w_ref[...], preferred_element_type=F32) + b_ref[...]
    o_ref[...] = h[:, :d] * jax.nn.sigmoid(h[:, d:])


def _pw1_glu(x, w_bf, b):
    t, d = x.shape
    tm = TM_PW1
    return pl.pallas_call(
        _pw1_glu_kernel,
        grid=(t // tm,),
        in_specs=[pl.BlockSpec((tm, d), lambda i: (i, 0)),
                  pl.BlockSpec((d, 2 * d), lambda i: (0, 0)),
                  pl.BlockSpec((1, 2 * d), lambda i: (0, 0))],
        out_specs=pl.BlockSpec((tm, d), lambda i: (i, 0)),
        out_shape=jax.ShapeDtypeStruct((t, d), F32),
        compiler_params=_cparams(("arbitrary",)),
        name="conv_pw1_glu",
    )(x, w_bf, b.reshape(1, -1))


def _conv_tail_kernel(hcur_ref, hprev_ref, x_ref, wdw_ref, bdw_ref, lng_ref, lnb_ref, wpw2_ref, bpw2_ref,
                      l1g_ref, l1b_ref, wrh_ref, wrl_ref, br_ref,
                      x1_ref, xs_ref, rg_ref, cw_ref,
                      hext_ref, hsh_ref, conv_ref):
    ts, d = hcur_ref.shape
    j = pl.program_id(1)

    hext_ref[HALO:, :] = hcur_ref[...]

    @pl.when(j == 0)
    def _():
        hext_ref[0:HALO, :] = jnp.zeros((HALO, d), F32)

    @pl.when(j > 0)
    def _():
        hext_ref[0:HALO, :] = hprev_ref[...]

    n_sh = ts + HALO - SUBLANES
    offs = [HALO - (CONV_WIDTH - 1) + k for k in range(CONV_WIDTH)]

    for c in range(d // LANES):
        lanes = slice(c * LANES, (c + 1) * LANES)
        for b in range(1, SUBLANES):
            hsh_ref[b - 1] = hext_ref[b:b + n_sh, lanes]
        w_rows = [jnp.broadcast_to(wdw_ref[k:k + 1, lanes], (SUBLANES, LANES)) for k in range(CONV_WIDTH)]
        bias = jnp.broadcast_to(bdw_ref[:, lanes], (SUBLANES, LANES))

        def conv_rows(r, carry, lanes=lanes, w_rows=w_rows, bias=bias):
            r0 = pl.multiple_of(r * CONV_ROWS, CONV_ROWS)
            for grp in range(CONV_ROWS // SUBLANES):
                acc = bias
                for k, off in enumerate(offs):
                    start = r0 + (grp + off // SUBLANES) * SUBLANES
                    if off % SUBLANES == 0:
                        tap = hext_ref[pl.ds(start, SUBLANES), lanes]
                    else:
                        tap = hsh_ref[off % SUBLANES - 1, pl.ds(start, SUBLANES), :]
                    acc = acc + w_rows[k] * tap
                conv_ref[pl.ds(r0 + grp * SUBLANES, SUBLANES), lanes] = acc
            return carry

        lax.fori_loop(0, ts // CONV_ROWS, conv_rows, 0)

    hn = _silu(_ln(conv_ref[...], lng_ref[...], lnb_ref[...]))
    mix = jnp.dot(hn.astype(BF16), wpw2_ref[...], preferred_element_type=F32) + bpw2_ref[...]
    pre = DEEPNORM_ALPHA * x_ref[...] + mix
    _ln1_route_sort(pre, l1g_ref, l1b_ref, wrh_ref, wrl_ref, br_ref, x1_ref, xs_ref, rg_ref, cw_ref)


def _conv_tail(h, x, batch, seq, w_dw, b_dw, ln_g, ln_b, w_pw2_bf, b_pw2, l1g, l1b, wrh, wrl, br):
    t, d = x.shape
    ts = TS_STEP
    nj = seq // ts
    halo_per_tile = ts // HALO
    h3 = h.reshape(batch, seq, d)
    x3 = x.reshape(batch, seq, d)
    row = lambda v: v.reshape(1, -1)
    const2 = lambda shape: pl.BlockSpec(shape, lambda b, j: (0, 0))
    tok = lambda b, j: (b * nj + j, 0)
    tok_t = lambda b, j: (0, b * nj + j)
    return pl.pallas_call(
        _conv_tail_kernel,
        grid=(batch, nj),
        in_specs=[pl.BlockSpec((None, ts, d), lambda b, j: (b, j, 0)),
                  pl.BlockSpec((None, HALO, d), lambda b, j: (b, jnp.maximum(j * halo_per_tile - 1, 0), 0)),
                  pl.BlockSpec((None, ts, d), lambda b, j: (b, j, 0)),
                  const2((CONV_WIDTH, d)), const2((1, d)), const2((1, d)), const2((1, d)),
                  const2((d, d)), const2((1, d)), const2((1, d)), const2((1, d)),
                  const2((ROUTE_ROWS, d)), const2((ROUTE_ROWS, d)), const2((ROUTE_ROWS, 1))],
        out_specs=_tail_out_specs(d, tok, tok_t),
        out_shape=_tail_out_shapes(t, d),
        scratch_shapes=[pltpu.VMEM((ts + HALO, d), F32), pltpu.VMEM((SUBLANES - 1, ts + HALO - SUBLANES, LANES), F32),
                        pltpu.VMEM((ts, d), F32)],
        compiler_params=_cparams(("arbitrary", "arbitrary")),
        name="conv_tail_ln1_route",
    )(h3, h3, x3, w_dw, row(b_dw), row(ln_g), row(ln_b), w_pw2_bf, row(b_pw2), row(l1g), row(l1b), wrh, wrl, br)


def _rope_table_kernel(pos_ref, invf_ref, cos_ref, sin_ref):
    ang = pos_ref[...].astype(F32) * invf_ref[...]
    cos_ref[...] = jnp.cos(ang)
    sin_ref[...] = jnp.sin(ang)


def _rope_tables(positions, half):
    t = positions.size
    tm = 1024
    inv_freq = ROPE_BASE ** (-jnp.arange(half, dtype=F32) / half)
    return pl.pallas_call(
        _rope_table_kernel,
        grid=(t // tm,),
        in_specs=[pl.BlockSpec((tm, 1), lambda i: (i, 0)), pl.BlockSpec((1, half), lambda i: (0, 0))],
        out_specs=(pl.BlockSpec((tm, half), lambda i: (i, 0)), pl.BlockSpec((tm, half), lambda i: (i, 0))),
        out_shape=(jax.ShapeDtypeStruct((t, half), F32), jax.ShapeDtypeStruct((t, half), F32)),
        compiler_params=_cparams(("arbitrary",)),
        name="rope_tables",
    )(positions.reshape(t, 1), inv_freq.reshape(1, half))


def _rotate(t, cos, sin, head_dim):
    half = head_dim // 2
    parts = []
    for h in range(t.shape[1] // head_dim):
        t1 = t[:, h * head_dim:h * head_dim + half]
        t2 = t[:, h * head_dim + half:(h + 1) * head_dim]
        parts.append(t1 * cos - t2 * sin)
        parts.append(t1 * sin + t2 * cos)
    return jnp.concatenate(parts, axis=-1)


def _qkvg_kernel(x_ref, wq_ref, wk_ref, wv_ref, wg_ref, cos_ref, sin_ref, xi_ref, zeta_ref,
                 q_ref, k_ref, v_ref, g_ref, qx_ref, kz_ref, *, head_qk):
    xb = x_ref[...].astype(BF16)
    cos = cos_ref[...]
    sin = sin_ref[...]
    q = _rotate(jnp.dot(xb, wq_ref[...], preferred_element_type=F32), cos, sin, head_qk)
    q_ref[...] = q.astype(BF16)
    qx_ref[...] = (q * xi_ref[...]).astype(BF16)
    k = _rotate(jnp.dot(xb, wk_ref[...], preferred_element_type=F32), cos, sin, head_qk) * (head_qk ** -0.5)
    k_ref[...] = k.astype(BF16)
    kz_ref[...] = (k * zeta_ref[...]).astype(BF16)
    v_ref[...] = jnp.dot(xb, wv_ref[...], preferred_element_type=F32).astype(BF16)
    g_ref[...] = jnp.dot(xb, wg_ref[...], preferred_element_type=F32).astype(BF16)


def _qkvg(x, w_bf, qk, vd, cos, sin):
    t, d = x.shape
    head_qk = qk // RET_HEADS
    tm = TM_QKVG
    assert vd == 2 * qk and tm == RET_BLOCK
    _, xi, zeta, _ = _ret_tables()
    per_col = lambda tab: jnp.repeat(tab[:, :, 0].T, head_qk, axis=1)
    tokb = lambda n: pl.BlockSpec((tm, n), lambda i: (i, 0))
    wcol = lambda n, j: pl.BlockSpec((d, n), lambda i: (0, j))
    const = pl.BlockSpec((tm, qk), lambda i: (0, 0))
    return pl.pallas_call(
        functools.partial(_qkvg_kernel, head_qk=head_qk),
        grid=(t // tm,),
        in_specs=[tokb(d), wcol(qk, 0), wcol(qk, 1), wcol(vd, 1), wcol(vd, 2), tokb(head_qk // 2), tokb(head_qk // 2),
                  const, const],
        out_specs=(tokb(qk), tokb(qk), tokb(vd), tokb(vd), tokb(qk), tokb(qk)),
        out_shape=(jax.ShapeDtypeStruct((t, qk), BF16), jax.ShapeDtypeStruct((t, qk), BF16),
                   jax.ShapeDtypeStruct((t, vd), BF16), jax.ShapeDtypeStruct((t, vd), BF16),
                   jax.ShapeDtypeStruct((t, qk), BF16), jax.ShapeDtypeStruct((t, qk), BF16)),
        compiler_params=_cparams(("arbitrary",)),
        name="ret_qkvg_rope",
    )(x, w_bf, w_bf, w_bf, w_bf, cos, sin, per_col(xi), per_col(zeta))


def _ret_core_kernel(q_ref, k_ref, v_ref, g_ref, qx_ref, kz_ref, mask_ref, dec_ref, gng_ref, gnb_ref,
                     y_ref, state_ref):
    blk = RET_BLOCK
    n_blk = q_ref.shape[0] // blk

    @pl.when(pl.program_id(2) == 0)
    def _():
        state_ref[...] = jnp.zeros_like(state_ref)

    mask = mask_ref[...]
    dec = dec_ref[0:1, 0:1]
    for n in range(n_blk):
        rows = slice(n * blk, (n + 1) * blk)
        q = q_ref[rows, :]
        k = k_ref[rows, :]
        v = v_ref[rows, :]
        state = state_ref[...]
        s = lax.dot_general(q, k, _NT, preferred_element_type=F32)
        p = (s * mask).astype(BF16)
        o = jnp.dot(p, v, preferred_element_type=F32)
        o = o + jnp.dot(qx_ref[rows, :], state.astype(BF16), preferred_element_type=F32)
        state_ref[...] = state * dec + lax.dot_general(kz_ref[rows, :], v, _TN, preferred_element_type=F32)
        mu = jnp.mean(o, axis=-1, keepdims=True)
        oc = o - mu
        var = jnp.mean(oc * oc, axis=-1, keepdims=True)
        on = oc * lax.rsqrt(var + LN_EPS) * gng_ref[...] + gnb_ref[...]
        y_ref[rows, :] = (_silu(g_ref[rows, :].astype(F32)) * on).astype(BF16)


def _ret_tables():
    blk = RET_BLOCK
    log_gamma = jnp.log(1.0 - 2.0 ** (-5.0 - jnp.arange(RET_HEADS, dtype=F32)))
    idx = jnp.arange(blk, dtype=F32)
    dist = jnp.abs(idx[:, None] - idx[None, :])
    visible = (jnp.floor(idx[None, :] / RET_CHUNK) <= jnp.floor(idx[:, None] / RET_CHUNK))
    mask = jnp.where(visible[None], jnp.exp(log_gamma[:, None, None] * dist[None]), 0.0)
    xi = jnp.exp(log_gamma[:, None] * (idx[None, :] + 1.0))[..., None]
    zeta = jnp.exp(log_gamma[:, None] * (blk - 1.0 - idx[None, :]))[..., None]
    dec = jnp.broadcast_to(jnp.exp(log_gamma * blk)[:, None, None], (RET_HEADS, SUBLANES, LANES))
    return mask.astype(F32), xi.astype(F32), zeta.astype(F32), dec.astype(F32)


def _ret_core(q, k, v, g, qx, kz, gn_g, gn_b, batch, seq):
    t, qk = q.shape
    vd = v.shape[1]
    hq = qk // RET_HEADS
    hv = vd // RET_HEADS
    sb = RET_SUPER
    ns = seq // sb
    mask, _, _, dec = _ret_tables()
    tokb = lambda n: pl.BlockSpec((sb, n), lambda b, h, s: (b * ns + s, h))
    headb = lambda r, c: pl.BlockSpec((None, r, c), lambda b, h, s: (h, 0, 0))
    return pl.pallas_call(
        _ret_core_kernel,
        grid=(batch, RET_HEADS, ns),
        in_specs=[tokb(hq), tokb(hq), tokb(hv), tokb(hv), tokb(hq), tokb(hq),
                  headb(RET_BLOCK, RET_BLOCK), headb(SUBLANES, LANES),
                  pl.BlockSpec((1, hv), lambda b, h, s: (0, h)), pl.BlockSpec((1, hv), lambda b, h, s: (0, h))],
        out_specs=tokb(hv),
        out_shape=jax.ShapeDtypeStruct((t, vd), BF16),
        scratch_shapes=[pltpu.VMEM((hq, hv), F32)],
        compiler_params=_cparams(("arbitrary", "arbitrary", "arbitrary")),
        name="ret_core",
    )(q, k, v, g, qx, kz, mask, dec, gn_g.reshape(1, -1), gn_b.reshape(1, -1))


def _ret_tail_kernel(y_ref, x_ref, wo_ref, l1g_ref, l1b_ref, wrh_ref, wrl_ref, br_ref,
                     x1_ref, xs_ref, rg_ref, cw_ref):
    mix = jnp.dot(y_ref[...], wo_ref[...], preferred_element_type=F32)
    pre = DEEPNORM_ALPHA * x_ref[...] + mix
    _ln1_route_sort(pre, l1g_ref, l1b_ref, wrh_ref, wrl_ref, br_ref, x1_ref, xs_ref, rg_ref, cw_ref)


def _ret_tail(y, x, w_o_bf, l1g, l1b, wrh, wrl, br):
    t, d = x.shape
    vd = y.shape[1]
    ts = TS_STEP
    row = lambda v: v.reshape(1, -1)
    const = lambda shape: pl.BlockSpec(shape, lambda i: (0, 0))
    tok = lambda i: (i, 0)
    tok_t = lambda i: (0, i)
    return pl.pallas_call(
        _ret_tail_kernel,
        grid=(t // ts,),
        in_specs=[pl.BlockSpec((ts, vd), tok), pl.BlockSpec((ts, d), tok), const((vd, d)),
                  const((1, d)), const((1, d)), const((ROUTE_ROWS, d)), const((ROUTE_ROWS, d)), const((ROUTE_ROWS, 1))],
        out_specs=_tail_out_specs(d, tok, tok_t),
        out_shape=_tail_out_shapes(t, d),
        compiler_params=_cparams(("arbitrary",)),
        name="ret_tail_ln1_route",
    )(y, x, w_o_bf, row(l1g), row(l1b), wrh, wrl, br)


def _ffn_kernel(gsrc_ref, tile_e_ref, tile_nv_ref, nt_ref, xs_hbm, wg_ref, wu_ref, wd_ref, ys_hbm,
                xbuf_ref, obuf_ref, wgb_ref, wub_ref, wdb_ref, sem_in, sem_out):
    del xs_hbm
    i = pl.program_id(0)
    n_steps = pl.num_programs(0)
    nt = nt_ref[0]
    slot = i % 2

    def groups(tile, buf, sem, to_hbm, wait):
        count = tile_nv_ref[tile]

        def copy(in_hbm, in_buf):
            return pltpu.make_async_copy(in_buf, in_hbm, sem) if to_hbm else pltpu.make_async_copy(in_hbm, in_buf, sem)

        if wait:
            @pl.when(count > 0)
            def _():
                rows = pl.ds(0, count * SUBLANES)
                copy(ys_hbm.at[rows], buf.at[rows]).wait()
            return

        def one(j, priority):
            copy(_row_group(ys_hbm, gsrc_ref[tile * TILE_GROUPS + j]),
                 _row_group(buf, j * SUBLANES)).start(priority=priority)

        def pair(jj, carry):
            one(2 * jj, 0)
            one(2 * jj + 1, 1)
            return carry

        @pl.when(count == TILE_GROUPS)
        def _():
            lax.fori_loop(0, TILE_GROUPS // 2, pair, 0, unroll=8)

        @pl.when(count < TILE_GROUPS)
        def _():
            lax.fori_loop(0, lax.shift_right_logical(count, 1), pair, 0)

            @pl.when((count & 1) == 1)
            def _():
                one(count - 1, 0)

    @pl.when(i == 0)
    def _():
        xbuf_ref[...] = jnp.zeros(xbuf_ref.shape, xbuf_ref.dtype)
        groups(0, xbuf_ref.at[0], sem_in.at[0], to_hbm=False, wait=False)

    @pl.when((i >= 2) & (i - 2 < nt))
    def _():
        groups(i - 2, obuf_ref.at[slot], sem_out.at[slot], to_hbm=True, wait=True)

    @pl.when(i < nt)
    def _():
        @pl.when((i == 0) | (tile_e_ref[i] != tile_e_ref[jnp.maximum(i - 1, 0)]))
        def _():
            wgb_ref[...] = wg_ref[...].astype(BF16)
            wub_ref[...] = wu_ref[...].astype(BF16)
            wdb_ref[...] = wd_ref[...].astype(BF16)

        groups(i, xbuf_ref.at[slot], sem_in.at[slot], to_hbm=False, wait=True)

        def compute_tile():
            x = _unpack_pairs(xbuf_ref[slot])
            a = jnp.dot(x, wgb_ref[...], preferred_element_type=F32)
            u = jnp.dot(x, wub_ref[...], preferred_element_type=F32)
            h = (_silu(a) * u).astype(BF16)
            obuf_ref[slot] = _pack_pairs(jnp.dot(h, wdb_ref[...], preferred_element_type=F32))

        def start_all(tile, buf, sem, to_hbm):
            for j in range(TILE_GROUPS):
                in_hbm = _row_group(ys_hbm, gsrc_ref[tile * TILE_GROUPS + j])
                in_buf = _row_group(buf, j * SUBLANES)
                cp = pltpu.make_async_copy(in_buf, in_hbm, sem) if to_hbm else pltpu.make_async_copy(in_hbm, in_buf, sem)
                cp.start(priority=j % 2)

        nxt = jnp.minimum(i + 1, n_steps - 1)
        full = (tile_nv_ref[i] == TILE_GROUPS) & (i + 1 < nt) & (tile_nv_ref[nxt] == TILE_GROUPS)

        @pl.when(full)
        def _():
            start_all(i + 1, xbuf_ref.at[1 - slot], sem_in.at[1 - slot], to_hbm=False)
            compute_tile()
            start_all(i, obuf_ref.at[slot], sem_out.at[slot], to_hbm=True)

        @pl.when(jnp.logical_not(full))
        def _():
            @pl.when(i + 1 < nt)
            def _():
                groups(i + 1, xbuf_ref.at[1 - slot], sem_in.at[1 - slot], to_hbm=False, wait=False)

            compute_tile()
            groups(i, obuf_ref.at[slot], sem_out.at[slot], to_hbm=True, wait=False)

    @pl.when((i == n_steps - 1) & (i >= 1) & (i - 1 < nt))
    def _():
        groups(i - 1, obuf_ref.at[1 - slot], sem_out.at[1 - slot], to_hbm=True, wait=True)

    @pl.when((i == n_steps - 1) & (i < nt))
    def _():
        groups(i, obuf_ref.at[slot], sem_out.at[slot], to_hbm=True, wait=True)


def _ffn(xs, gsrc, tile_e, tile_nv, n_tiles, w_gate, w_up, w_down, layer, nt_max):
    d, f = w_gate.shape[-2:]
    assert xs.shape[1] * 2 == d
    tm = TM_FFN
    wspec = lambda r, c: pl.BlockSpec((None, None, r, c), lambda i, gs, te, nv, nt: (layer, te[i], 0, 0))
    grid_spec = pltpu.PrefetchScalarGridSpec(
        num_scalar_prefetch=4,
        grid=(nt_max,),
        in_specs=[pl.BlockSpec(memory_space=pl.ANY), wspec(d, f), wspec(d, f), wspec(f, d)],
        out_specs=pl.BlockSpec(memory_space=pl.ANY),
        scratch_shapes=[pltpu.VMEM((2, tm, d // 2), I32), pltpu.VMEM((2, tm, d // 2), I32),
                        pltpu.VMEM((d, f), BF16), pltpu.VMEM((d, f), BF16), pltpu.VMEM((f, d), BF16),
                        pltpu.SemaphoreType.DMA((2,)), pltpu.SemaphoreType.DMA((2,))],
    )
    return pl.pallas_call(
        _ffn_kernel,
        grid_spec=grid_spec,
        out_shape=jax.ShapeDtypeStruct(xs.shape, xs.dtype),
        input_output_aliases={4: 0},
        compiler_params=_cparams(("arbitrary",)),
        name="moe_expert_ffn",
    )(gsrc, tile_e, tile_nv, n_tiles, xs, w_gate, w_up, w_down)


def _combine_ln2_kernel(x1_ref, rg_ref, ys_ref, l2g_ref, l2b_ref, o_ref):
    ts = TS_TAIL
    srow = lax.broadcasted_iota(I32, (WIN_ROWS, ts), 0).astype(F32)
    for w in range(o_ref.shape[0] // ts):
        rows = slice(w * ts, (w + 1) * ts)
        info = rg_ref[:, rows]
        g0, g1, lp0, lp1 = info[0:1, :], info[1:2, :], info[2:3, :], info[3:4, :]
        unsort = (jnp.where(srow == lp0, g0, 0.0) + jnp.where(srow == lp1, g1, 0.0)).astype(BF16)
        ys = _unpack_pairs(ys_ref[w * WIN_ROWS:(w + 1) * WIN_ROWS, :])
        ffn = lax.dot_general(unsort, ys, _TN, preferred_element_type=F32)
        o_ref[rows, :] = _ln(DEEPNORM_ALPHA * x1_ref[rows, :] + ffn, l2g_ref[...], l2b_ref[...])


def _combine_ln2(x1, rg, ys, l2g, l2b):
    t, d = x1.shape
    ts = TS_STEP
    return pl.pallas_call(
        _combine_ln2_kernel,
        grid=(t // ts,),
        in_specs=[pl.BlockSpec((ts, d), lambda i: (i, 0)),
                  pl.BlockSpec((SUBLANES, ts), lambda i: (0, i)),
                  pl.BlockSpec((TAIL_WINDOWS * WIN_ROWS, d // 2), lambda i: (i, 0)),
                  pl.BlockSpec((1, d), lambda i: (0, 0)),
                  pl.BlockSpec((1, d), lambda i: (0, 0))],
        out_specs=pl.BlockSpec((ts, d), lambda i: (i, 0)),
        out_shape=jax.ShapeDtypeStruct((t, d), F32),
        compiler_params=_cparams(("arbitrary",)),
        name="moe_combine_ln2",
    )(x1, rg, ys, l2g.reshape(1, -1), l2b.reshape(1, -1))


def _pick_last(below, values, axis):
    first = lax.index_in_dim(values, 0, axis, keepdims=False)
    steps = lax.slice_in_dim(values, 1, None, axis=axis) - lax.slice_in_dim(values, 0, -1, axis=axis)
    return first + jnp.sum(jnp.where(below, steps, 0), axis=axis)


def _moe_tables(cw, nw, nt_max):
    tm = TM_FFN
    cnt = cw.reshape(nw, ROUTE_ROWS, SUBLANES)[:, ROUTE_ROW0:ROUTE_ROW0 + N_EXPERTS, 0].astype(I32)
    run = ((cnt + SUBLANES - 1) // SUBLANES) * SUBLANES
    loc = jnp.cumsum(run, axis=1) - run
    e_rows = jnp.sum(run, axis=0)
    e_pad = ((e_rows + tm - 1) // tm) * tm
    e_end = jnp.cumsum(e_pad)
    e_off = e_end - e_pad
    glob = e_off[None, :] + jnp.cumsum(run, axis=0) - run

    tile_start = jnp.arange(nt_max, dtype=I32) * tm
    done = e_end[None, :-1] <= tile_start[:, None]
    tile_e = jnp.sum(done.astype(I32), axis=1)
    real_end = _pick_last(done, (e_off + e_rows)[None, :], 1)
    tile_nv = jnp.clip(real_end - tile_start, 0, tm) // SUBLANES

    of_tile = lambda v: _pick_last(done[:, None, :], v[None, :, :], 2)
    win_row0 = jnp.arange(nw, dtype=I32)[:, None] * WIN_ROWS
    t_start, t_end, t_shift = of_tile(glob), of_tile(glob + run), of_tile(win_row0 + loc - glob)
    rows = tile_start[:, None] + jnp.arange(TILE_GROUPS, dtype=I32)[None, :] * SUBLANES
    below = t_start[:, None, 1:] <= rows[:, :, None]
    zero_group = WIN_ROWS - SUBLANES
    gsrc = jnp.where(rows < _pick_last(below, t_end[:, None, :], 2), rows + _pick_last(below, t_shift[:, None, :], 2),
                     zero_group).astype(I32).reshape(-1)
    n_tiles = (e_end[-1:] // tm).astype(I32)
    return gsrc, tile_e.astype(I32), tile_nv.astype(I32), n_tiles


def _moe(x1, xs, rg, cw, w_gate, w_up, w_down, layer, l2g, l2b):
    t = x1.shape[0]
    nw = t // TS_TAIL
    max_rows = TOP_K * t + nw * N_EXPERTS * (SUBLANES - 1) + N_EXPERTS * (TM_FFN - 1)
    nt_max = -(-max_rows // TM_FFN)
    gsrc, tile_e, tile_nv, n_tiles = _moe_tables(cw, nw, nt_max)
    ys = _ffn(xs, gsrc, tile_e, tile_nv, n_tiles, w_gate, w_up, w_down, layer, nt_max)
    return _combine_ln2(x1, rg, ys, l2g, l2b)


def _router_weights(w_grp, b_grp, w_route, b_route):
    d = w_grp.shape[0]
    used = N_GROUPS + N_EXPERTS
    w = jnp.concatenate([w_grp, w_route, jnp.zeros((d, ROUTE_ROWS - used), F32)], axis=1).T
    b = jnp.concatenate([b_grp, b_route, jnp.zeros((ROUTE_ROWS - used,), F32)]).reshape(ROUTE_ROWS, 1)
    wh = w.astype(BF16)
    wl = (w - wh.astype(F32)).astype(BF16)
    return wh, wl, b


def kernel(x, positions, conv_w_pw1, conv_b_pw1, conv_w_dw, conv_b_dw, conv_ln_g, conv_ln_b, conv_w_pw2, conv_b_pw2,
           ret_w_qkvg, ret_gn_g, ret_gn_b, ret_w_o, ln1_g, ln1_b, ln2_g, ln2_b, moe_w_grp, moe_b_grp, moe_w_route,
           moe_b_route, moe_w_gate, moe_w_up, moe_w_down):
    batch, seq, d = x.shape
    t = batch * seq
    qk = d
    vd = 2 * d
    xt = x.reshape(t, d)
    cos, sin = _rope_tables(positions, qk // RET_HEADS // 2)
    for i in range(DEPTH):
        j = i // N_MIXERS
        wrh, wrl, br = _router_weights(moe_w_grp[i], moe_b_grp[i], moe_w_route[i], moe_b_route[i])
        if i % N_MIXERS == 0:
            h = _pw1_glu(xt, conv_w_pw1[j].astype(BF16), conv_b_pw1[j])
            x1, xs, rg, cw = _conv_tail(h, xt, batch, seq, conv_w_dw[j], conv_b_dw[j], conv_ln_g[j], conv_ln_b[j],
                                        conv_w_pw2[j].astype(BF16), conv_b_pw2[j], ln1_g[i], ln1_b[i], wrh, wrl, br)
        else:
            q, k, v, g, qx, kz = _qkvg(xt, ret_w_qkvg[j].astype(BF16), qk, vd, cos, sin)
            y = _ret_core(q, k, v, g, qx, kz, ret_gn_g[j], ret_gn_b[j], batch, seq)
            x1, xs, rg, cw = _ret_tail(y, xt, ret_w_o[j].astype(BF16), ln1_g[i], ln1_b[i], wrh, wrl, br)
        xt = _moe(x1, xs, rg, cw, moe_w_gate, moe_w_up, moe_w_down, i, ln2_g[i], ln2_b[i])
    return xt.reshape(batch, seq, d)
```
